```python
import math
import jax, jax.numpy as jnp
from jax import lax
import numpy as np

D_MODEL = 1024
BATCH = 8
SEQ = 2048
DEPTH = 1

CHUNK = 64
Q_BLOCK = 128
HEAD_DIM = D_MODEL // 16
N_HEADS_DIFF = D_MODEL // 256
N_HEADS_SB = D_MODEL // 128
DIFF_QK_WIDTH = N_HEADS_DIFF * 2 * HEAD_DIM
DIFF_V_WIDTH = N_HEADS_DIFF * 2 * HEAD_DIM
SB_WIDTH = N_HEADS_SB * HEAD_DIM
N_BRANCHES = 2
IN_WIDTH = 2 * DIFF_QK_WIDTH + DIFF_V_WIDTH + 3 * SB_WIDTH + N_BRANCHES * D_MODEL
D_FF = ((8 * D_MODEL // 3 + 255) // 256) * 256
CONV_WIDTH = 3
N_BUCKETS = 32
MAX_DISTANCE = 128
NORM_EPS = 1e-6

kernel_name = "hybrid_diff_stickbreak_convffn_block"


def rms_norm(x, g):
    xf = x.astype(jnp.float32)
    y = xf * lax.rsqrt(jnp.mean(xf * xf, axis=-1, keepdims=True) + NORM_EPS)
    return (y * g.astype(jnp.float32)).astype(x.dtype)


def t5_bucket(rel):
    half = N_BUCKETS // 2
    ret = jnp.where(rel > 0, half, 0)
    n = jnp.abs(rel)
    max_exact = half // 2
    nf = jnp.maximum(n, 1).astype(jnp.float32)
    large = max_exact + (jnp.log(nf / max_exact) / math.log(MAX_DISTANCE / max_exact)
                         * (half - max_exact)).astype(jnp.int32)
    large = jnp.minimum(large, half - 1)
    return ret + jnp.where(n < max_exact, n, large)


def differential_attention(q, k, v, rel_bias, lam, subln_g, lambda_init):
    B, S = q.shape[0], q.shape[1]
    scale = HEAD_DIM ** -0.5
    outs = []
    for t0 in range(0, S, Q_BLOCK):
        kl = t0 + Q_BLOCK
        qb, kb, vb = q[:, t0:kl], k[:, :kl], v[:, :kl]
        s = jnp.einsum('bqhnd,bkhnd->bnhqk', qb, kb).astype(jnp.float32) * scale
        qpos = jnp.arange(t0, kl)
        kpos = jnp.arange(kl)
        bias = rel_bias[t5_bucket(kpos[None, :] - qpos[:, None])]
        bias = jnp.transpose(bias, (2, 0, 1)).astype(jnp.float32)
        mask = (kpos[None, :] // CHUNK) <= (qpos[:, None] // CHUNK)
        s = jnp.where(mask, s + bias, -jnp.inf)
        p = jax.nn.softmax(s, axis=-1)
        a = p[:, 0] - lam * p[:, 1]
        outs.append(jnp.einsum('bhqk,bkhe->bqhe', a, vb.astype(jnp.float32)))
    o = jnp.concatenate(outs, axis=1)
    o = o * lax.rsqrt(jnp.mean(o * o, axis=-1, keepdims=True) + NORM_EPS)
    o = o * subln_g.astype(jnp.float32) * (1.0 - lambda_init)
    return o.reshape(B, S, DIFF_V_WIDTH).astype(q.dtype)


def stick_breaking_attention(q, k, v):
    B, S = q.shape[0], q.shape[1]
    scale = HEAD_DIM ** -0.5
    outs = []
    for t0 in range(0, S, Q_BLOCK):
        kl = t0 + Q_BLOCK
        qb, kb, vb = q[:, t0:kl], k[:, :kl], v[:, :kl]
        z = jnp.einsum('bqhd,bkhd->bhqk', qb, kb).astype(jnp.float32) * scale
        qpos = jnp.arange(t0, kl)
        kpos = jnp.arange(kl)
        causal = kpos[None, :] < qpos[:, None]
        log_beta = jax.nn.log_sigmoid(z)
        log_keep = jnp.where(causal, jax.nn.log_sigmoid(-z), 0.0)
        after = lax.cumsum(log_keep, axis=3, reverse=True) - log_keep
        w = jnp.where(causal, jnp.exp(log_beta + after), 0.0)
        outs.append(jnp.einsum('bhqk,bkhd->bqhd', w, vb.astype(jnp.float32)))
    o = jnp.concatenate(outs, axis=1)
    return o.reshape(B, S, SB_WIDTH).astype(q.dtype)


def causal_depthwise_conv(a, w, b):
    F = a.shape[-1]
    y = lax.conv_general_dilated(a, w[:, None, :].astype(a.dtype), window_strides=(1,),
                                 padding=[(CONV_WIDTH - 1, 0)],
                                 dimension_numbers=('NWC', 'WIO', 'NWC'),
                                 feature_group_count=F)
    return y + b.astype(a.dtype)


def setup_inputs(seed: int = 0) -> dict:
    key = jax.random.key(seed)
    ks = jax.random.split(key, 20)
    f32 = jnp.float32
    nrm = lambda k, shape, s: jax.random.normal(k, shape, f32) * s
    return {
        "x": nrm(ks[0], (BATCH, SEQ, D_MODEL), 1.0),
        "norm_mix_g": 1.0 + nrm(ks[1], (DEPTH, D_MODEL), 0.02),
        "w_in": nrm(ks[2], (DEPTH, D_MODEL, IN_WIDTH), D_MODEL ** -0.5),
        "diff_lambda_q1": nrm(ks[3], (DEPTH, HEAD_DIM), 0.1),
        "diff_lambda_k1": nrm(ks[4], (DEPTH, HEAD_DIM), 0.1),
        "diff_lambda_q2": nrm(ks[5], (DEPTH, HEAD_DIM), 0.1),
        "diff_lambda_k2": nrm(ks[6], (DEPTH, HEAD_DIM), 0.1),
        "diff_subln_g": 1.0 + nrm(ks[7], (DEPTH, 2 * HEAD_DIM), 0.02),
        "rel_bias": nrm(ks[8], (N_BUCKETS, N_HEADS_DIFF), 0.5),
        "w_branch_diff": nrm(ks[9], (DEPTH, DIFF_V_WIDTH, D_MODEL), DIFF_V_WIDTH ** -0.5),
        "w_branch_sb": nrm(ks[10], (DEPTH, SB_WIDTH, D_MODEL), SB_WIDTH ** -0.5),
        "w_out": nrm(ks[11], (DEPTH, D_MODEL, D_MODEL), D_MODEL ** -0.5),
        "norm_ffn_g": 1.0 + nrm(ks[12], (DEPTH, D_MODEL), 0.02),
        "w_ffn_up": nrm(ks[13], (DEPTH, D_MODEL, 2 * D_FF), D_MODEL ** -0.5),
        "ffn_conv_w": nrm(ks[14], (DEPTH, CONV_WIDTH, D_FF), CONV_WIDTH ** -0.5),
        "ffn_conv_b": nrm(ks[15], (DEPTH, D_FF), 0.02),
        "w_ffn_down": nrm(ks[16], (DEPTH, D_FF, D_MODEL), D_FF ** -0.5),
        "norm_final_g": 1.0 + nrm(ks[17], (D_MODEL,), 0.02),
    }


def reference(x, norm_mix_g, w_in, diff_lambda_q1, diff_lambda_k1, diff_lambda_q2, diff_lambda_k2,
              diff_subln_g, rel_bias, w_branch_diff, w_branch_sb, w_out, norm_ffn_g, w_ffn_up,
              ffn_conv_w, ffn_conv_b, w_ffn_down, norm_final_g):
    B, S = x.shape[0], x.shape[1]
    for layer in range(DEPTH):
        lambda_init = 0.8 - 0.6 * math.exp(-0.3 * layer)
        h = rms_norm(x, norm_mix_g[layer])
        proj = h @ w_in[layer]
        o = 0
        dq = proj[..., o:o + DIFF_QK_WIDTH].reshape(B, S, N_HEADS_DIFF, 2, HEAD_DIM); o += DIFF_QK_WIDTH
        dk = proj[..., o:o + DIFF_QK_WIDTH].reshape(B, S, N_HEADS_DIFF, 2, HEAD_DIM); o += DIFF_QK_WIDTH
        dv = proj[..., o:o + DIFF_V_WIDTH].reshape(B, S, N_HEADS_DIFF, 2 * HEAD_DIM); o += DIFF_V_WIDTH
        sq = proj[..., o:o + SB_WIDTH].reshape(B, S, N_HEADS_SB, HEAD_DIM); o += SB_WIDTH
        sk = proj[..., o:o + SB_WIDTH].reshape(B, S, N_HEADS_SB, HEAD_DIM); o += SB_WIDTH
        sv = proj[..., o:o + SB_WIDTH].reshape(B, S, N_HEADS_SB, HEAD_DIM); o += SB_WIDTH
        gates = jax.nn.sigmoid(proj[..., o:o + N_BRANCHES * D_MODEL].reshape(B, S, N_BRANCHES, D_MODEL))

        lam = (jnp.exp(jnp.sum(diff_lambda_q1[layer].astype(jnp.float32) * diff_lambda_k1[layer].astype(jnp.float32)))
               - jnp.exp(jnp.sum(diff_lambda_q2[layer].astype(jnp.float32) * diff_lambda_k2[layer].astype(jnp.float32)))
               + lambda_init)
        y_diff = differential_attention(dq, dk, dv, rel_bias, lam, diff_subln_g[layer], lambda_init)
        y_sb = stick_breaking_attention(sq, sk, sv)

        merged = (gates[:, :, 0] * (y_diff @ w_branch_diff[layer])
                  + gates[:, :, 1] * (y_sb @ w_branch_sb[layer]))
        x = x + merged @ w_out[layer]

        h = rms_norm(x, norm_ffn_g[layer])
        up = h @ w_ffn_up[layer]
        a = causal_depthwise_conv(up[..., :D_FF], ffn_conv_w[layer], ffn_conv_b[layer])
        b = up[..., D_FF:]
        x = x + (jax.nn.gelu(a, approximate=False) * b) @ w_ffn_down[layer]
    return rms_norm(x, norm_final_g)
```

```python
import functools
import math

import jax
import jax.numpy as jnp
from jax import lax
from jax.experimental import pallas as pl
from jax.experimental.pallas import tpu as pltpu

F32 = jnp.float32
BF16 = jnp.bfloat16

HEAD_DIM = 64
CHUNK = 64
N_BUCKETS = 32
MAX_DISTANCE = 128
NORM_EPS = 1e-6
CONV_WIDTH = 3

LANES = 128
SUBLANES = 8
ATT_TILE = 256
VMEM_LIMIT = 56 * 1024 * 1024


def _dot(a, b):
    return jnp.dot(a, b, preferred_element_type=F32)


def _rms(x, g):
    return x * lax.rsqrt(jnp.mean(x * x, axis=-1, keepdims=True) + NORM_EPS) * g


def _proj_kernel(x_ref, g_ref, wk_ref, wqvt_ref, k_ref, qt_ref, vt_ref, *, n_groups, scale):
    h = _rms(x_ref[0], g_ref[...]).astype(BF16)
    kk = _dot(h, wk_ref[...]).astype(BF16)
    for g in range(n_groups):
        k_ref[0, g] = kk[:, g * LANES:(g + 1) * LANES]
    qv = lax.dot_general(wqvt_ref[...], h, (((1,), (1,)), ((), ())), preferred_element_type=F32)
    nq = n_groups * LANES
    for g in range(n_groups):
        qt_ref[0, 0, g] = (qv[g * LANES:(g + 1) * LANES] * scale).astype(BF16)
        vt_ref[0, 0, g] = qv[nq + g * LANES:nq + (g + 1) * LANES].astype(BF16)


def _proj(x, g, wk, wqvt, n_groups):
    B, S, D = x.shape
    T = ATT_TILE
    nt = S // T
    kern = functools.partial(_proj_kernel, n_groups=n_groups, scale=HEAD_DIM ** -0.5)
    return pl.pallas_call(
        kern,
        grid=(B, nt),
        in_specs=[
            pl.BlockSpec((1, T, D), lambda b, i: (b, i, 0)),
            pl.BlockSpec((1, D), lambda b, i: (0, 0)),
            pl.BlockSpec(wk.shape, lambda b, i: (0, 0)),
            pl.BlockSpec(wqvt.shape, lambda b, i: (0, 0)),
        ],
        out_specs=[
            pl.BlockSpec((1, n_groups, T, LANES), lambda b, i: (b, 0, i, 0)),
            pl.BlockSpec((1, 1, n_groups, LANES, T), lambda b, i: (b, i, 0, 0, 0)),
            pl.BlockSpec((1, 1, n_groups, LANES, T), lambda b, i: (b, i, 0, 0, 0)),
        ],
        out_shape=[
            jax.ShapeDtypeStruct((B, n_groups, S, LANES), BF16),
            jax.ShapeDtypeStruct((B, nt, n_groups, LANES, T), BF16),
            jax.ShapeDtypeStruct((B, nt, n_groups, LANES, T), BF16),
        ],
        compiler_params=pltpu.CompilerParams(
            dimension_semantics=("arbitrary", "arbitrary"), vmem_limit_bytes=VMEM_LIMIT),
        name="proj",
    )(x, g, wk, wqvt)


def _block_diag_q(qg):
    row = lax.broadcasted_iota(jnp.int32, qg.shape, 0)
    zero = jnp.zeros_like(qg)
    return jnp.concatenate([jnp.where(row < HEAD_DIM, qg, zero), jnp.where(row >= HEAD_DIM, qg, zero)], axis=1)


def _key_tile(k_ref, g, j):
    T = ATT_TILE
    return k_ref[0, g, pl.ds(pl.multiple_of(j * T, T), T), :]


def _diff_kernel(lq1_ref, lk1_ref, lq2_ref, lk2_ref, gs_ref, nbd_ref, nbp_ref, qt_ref, k_ref, vt_ref,
                 y_ref, s_ref, mx_ref, l_ref, ot_ref, *, n_groups, lambda_init):
    T = ATT_TILE
    qi = pl.program_id(1)
    lam = (jnp.exp(jnp.sum(lq1_ref[...] * lk1_ref[...], keepdims=True))
           - jnp.exp(jnp.sum(lq2_ref[...] * lk2_ref[...], keepdims=True)) + lambda_init)

    for g in range(n_groups):
        qbd = _block_diag_q(qt_ref[0, 0, g])

        def scores(j, nb_ref):
            st = _dot(_key_tile(k_ref, g, j), qbd)
            if nb_ref is not None:
                nb = nb_ref[g]
                st = jnp.concatenate([st[:, :T] + nb, st[:, T:] + nb], axis=1)
            s_ref[j] = st
            return jnp.max(st, axis=0, keepdims=True)

        mx_ref[...] = scores(qi, nbd_ref)

        @pl.when(qi > 0)
        def _():
            mx_ref[...] = jnp.maximum(mx_ref[...], scores(qi - 1, nbp_ref))

        def far_body(j, carry):
            mx_ref[...] = jnp.maximum(mx_ref[...], scores(j, None))
            return carry
        lax.fori_loop(0, jnp.maximum(qi - 1, 0), far_body, 0)

        m = mx_ref[...]
        l_ref[...] = jnp.zeros_like(l_ref)
        ot_ref[...] = jnp.zeros_like(ot_ref)

        def pv_body(j, carry):
            p = jnp.exp(s_ref[j] - m)
            l_ref[...] += jnp.sum(p, axis=0, keepdims=True)
            ot_ref[...] += _dot(vt_ref[0, j, g], p.astype(BF16))
            return carry
        lax.fori_loop(0, qi + 1, pv_body, 0)

        ot = ot_ref[...] * (1.0 / l_ref[...])
        o = ot[:, :T] - lam * ot[:, T:]
        o = o * lax.rsqrt(jnp.mean(o * o, axis=0, keepdims=True) + NORM_EPS)
        y = o.T * gs_ref[...] * (1.0 - lambda_init)
        y_ref[0, :, g * LANES:(g + 1) * LANES] = y.astype(y_ref.dtype)


def _diff_attention(qt, k, vt, near_diag, near_prev, lq1, lk1, lq2, lk2, gs, lambda_init, n_groups):
    B, _, S, _ = k.shape
    T = ATT_TILE
    nt = S // T
    kern = functools.partial(_diff_kernel, n_groups=n_groups, lambda_init=lambda_init)
    small = lambda a: pl.BlockSpec(a.shape, lambda b, i: (0,) * a.ndim)
    return pl.pallas_call(
        kern,
        grid=(B, nt),
        in_specs=[
            small(lq1), small(lk1), small(lq2), small(lk2), small(gs), small(near_diag), small(near_prev),
            pl.BlockSpec((1, 1, n_groups, LANES, T), lambda b, i: (b, i, 0, 0, 0)),
            pl.BlockSpec((1, n_groups, S, LANES), lambda b, i: (b, 0, 0, 0)),
            pl.BlockSpec((1, nt, n_groups, LANES, T), lambda b, i: (b, 0, 0, 0, 0)),
        ],
        out_specs=pl.BlockSpec((1, T, n_groups * LANES), lambda b, i: (b, i, 0)),
        out_shape=jax.ShapeDtypeStruct((B, S, n_groups * LANES), BF16),
        scratch_shapes=[
            pltpu.VMEM((nt, T, 2 * T), F32),
            pltpu.VMEM((1, 2 * T), F32),
            pltpu.VMEM((1, 2 * T), F32),
            pltpu.VMEM((LANES, 2 * T), F32),
        ],
        compiler_params=pltpu.CompilerParams(
            dimension_semantics=("arbitrary", "arbitrary"), vmem_limit_bytes=VMEM_LIMIT),
        name="diff_attn",
    )(lq1, lk1, lq2, lk2, gs, near_diag, near_prev, qt, k, vt)


def _sb_kernel(uu_ref, qt_ref, k_ref, vt_ref, y_ref, c_ref, ot_ref, *, n_groups):
    T = ATT_TILE
    qi = pl.program_id(1)
    kk = lax.broadcasted_iota(jnp.int32, (T, 2 * T), 0)
    qq = lax.broadcasted_iota(jnp.int32, (T, 2 * T), 1)
    causal = kk < jnp.where(qq >= T, qq - T, qq)

    for g in range(n_groups):
        qbd = _block_diag_q(qt_ref[0, 0, g])
        c_ref[...] = jnp.zeros_like(c_ref)
        ot_ref[...] = jnp.zeros_like(ot_ref)

        def tile(j, masked):
            z = _dot(_key_tile(k_ref, g, j), qbd)
            lp = jnp.log(1.0 + jnp.exp(-jnp.abs(z)))
            log_beta = jnp.minimum(z, 0.0) - lp
            log_keep = log_beta - z
            if masked:
                log_keep = jnp.where(causal, log_keep, 0.0)
            hi = log_keep.astype(BF16)
            lo = (log_keep - hi.astype(F32)).astype(BF16)
            after = _dot(uu_ref[...], jnp.concatenate([hi, lo], axis=0)) + c_ref[...]
            w = jnp.exp(log_beta + after)
            if masked:
                w = jnp.where(causal, w, 0.0)
            ot_ref[...] += _dot(vt_ref[0, j, g], w.astype(BF16))
            c_ref[...] = after[0:1] + log_keep[0:1]

        tile(qi, True)

        def body(t, carry):
            tile(qi - 1 - t, False)
            return carry
        lax.fori_loop(0, qi, body, 0)

        ot = ot_ref[...]
        o = jnp.concatenate([ot[:HEAD_DIM, :T], ot[HEAD_DIM:, T:]], axis=0)
        y_ref[0, :, g * LANES:(g + 1) * LANES] = o.T.astype(y_ref.dtype)


def _sb_attention(qt, k, vt, uu, n_groups, group_block):
    B, _, S, _ = k.shape
    T = ATT_TILE
    nt = S // T
    gb = group_block
    kern = functools.partial(_sb_kernel, n_groups=n_groups)
    return pl.pallas_call(
        kern,
        grid=(B, nt),
        in_specs=[
            pl.BlockSpec(uu.shape, lambda b, i: (0, 0)),
            pl.BlockSpec((1, 1, n_groups, LANES, T), lambda b, i: (b, i, gb, 0, 0)),
            pl.BlockSpec((1, n_groups, S, LANES), lambda b, i: (b, gb, 0, 0)),
            pl.BlockSpec((1, nt, n_groups, LANES, T), lambda b, i: (b, 0, gb, 0, 0)),
        ],
        out_specs=pl.BlockSpec((1, T, n_groups * LANES), lambda b, i: (b, i, 0)),
        out_shape=jax.ShapeDtypeStruct((B, S, n_groups * LANES), BF16),
        scratch_shapes=[
            pltpu.VMEM((1, 2 * T), F32),
            pltpu.VMEM((LANES, 2 * T), F32),
        ],
        compiler_params=pltpu.CompilerParams(
            dimension_semantics=("arbitrary", "arbitrary"), vmem_limit_bytes=VMEM_LIMIT),
        name="sb_attn",
    )(uu, qt, k, vt)


def _mix_kernel(x_ref, yd_ref, ys_ref, g_ref, wg_ref, wbd_ref, wbs_ref, wo_ref, o_ref):
    x = x_ref[...]
    D = x.shape[-1]
    h = _rms(x, g_ref[...]).astype(BF16)
    gates = jax.nn.sigmoid(_dot(h, wg_ref[...]))
    merged = gates[:, :D] * _dot(yd_ref[...], wbd_ref[...]) + gates[:, D:] * _dot(ys_ref[...], wbs_ref[...])
    o_ref[...] = x + _dot(merged.astype(BF16), wo_ref[...])


def _mix(x2d, yd, ys, g, wg, wbd, wbs, wo, tm):
    N, D = x2d.shape
    const = lambda a: pl.BlockSpec(a.shape, lambda i: (0, 0))
    row = lambda w: pl.BlockSpec((tm, w), lambda i: (i, 0))
    return pl.pallas_call(
        _mix_kernel,
        grid=(N // tm,),
        in_specs=[row(D), row(yd.shape[1]), row(ys.shape[1]), const(g), const(wg), const(wbd), const(wbs), const(wo)],
        out_specs=row(D),
        out_shape=jax.ShapeDtypeStruct((N, D), F32),
        compiler_params=pltpu.CompilerParams(dimension_semantics=("arbitrary",), vmem_limit_bytes=VMEM_LIMIT),
        name="mix",
    )(x2d, yd, ys, g, wg, wbd, wbs, wo)


def _ffn_kernel(x_ref, g_ref, wup_ref, cw_ref, cb_ref, wdn_ref, gf_ref, o_ref, abuf_ref, *, final_norm):
    tm = x_ref.shape[1]
    F = cw_ref.shape[1]
    H = SUBLANES

    @pl.when(pl.program_id(1) == 0)
    def _():
        abuf_ref[0:H] = jnp.zeros((H, F), F32)

    x = x_ref[0]
    h = _rms(x, g_ref[...]).astype(BF16)
    up = _dot(h, wup_ref[...])
    a = up[:, :F]
    abuf_ref[H:H + tm] = a
    conv = (cw_ref[0:1] * abuf_ref[H - 2:H - 2 + tm] + cw_ref[1:2] * abuf_ref[H - 1:H - 1 + tm]
            + cw_ref[2:3] * a + cb_ref[...])
    abuf_ref[0:H] = abuf_ref[tm:tm + H]
    gelu = 0.5 * conv * (1.0 + lax.erf(conv * (2.0 ** -0.5)))
    act = gelu * up[:, F:]
    x = x + _dot(act.astype(BF16), wdn_ref[...])
    o_ref[0] = _rms(x, gf_ref[...]) if final_norm else x


def _ffn(x, g, wup, cw, cb, wdn, gf, tm, final_norm):
    B, S, D = x.shape
    F = cw.shape[1]
    const = lambda a: pl.BlockSpec(a.shape, lambda b, i: (0, 0), pipeline_mode=pl.Buffered(1))
    return pl.pallas_call(
        functools.partial(_ffn_kernel, final_norm=final_norm),
        grid=(B, S // tm),
        in_specs=[pl.BlockSpec((1, tm, D), lambda b, i: (b, i, 0)),
                  const(g), const(wup), const(cw), const(cb), const(wdn), const(gf)],
        out_specs=pl.BlockSpec((1, tm, D), lambda b, i: (b, i, 0)),
        out_shape=jax.ShapeDtypeStruct((B, S, D), F32),
        scratch_shapes=[pltpu.VMEM((tm + SUBLANES, F), F32)],
        compiler_params=pltpu.CompilerParams(
            dimension_semantics=("arbitrary", "arbitrary"), vmem_limit_bytes=VMEM_LIMIT),
        name="ffn",
    )(x, g, wup, cw, cb, wdn, gf)


def _t5_bucket(rel):
    half = N_BUCKETS // 2
    ret = jnp.where(rel > 0, half, 0)
    n = jnp.abs(rel)
    max_exact = half // 2
    nf = jnp.maximum(n, 1).astype(jnp.float32)
    large = max_exact + (jnp.log(nf / max_exact) / math.log(MAX_DISTANCE / max_exact)
                         * (half - max_exact)).astype(jnp.int32)
    large = jnp.minimum(large, half - 1)
    return ret + jnp.where(n < max_exact, n, large)


def _near_bias_tables(rel_bias):
    T = ATT_TILE
    kpos = jnp.arange(T)[:, None]
    qpos = jnp.arange(T)[None, :]
    far = rel_bias[_t5_bucket(jnp.asarray(-MAX_DISTANCE))].astype(F32)
    def table(key_offset):
        rel = kpos + key_offset - qpos
        b = rel_bias[_t5_bucket(rel)].astype(F32) - far
        return jnp.transpose(b, (2, 0, 1))
    visible = (kpos // CHUNK) <= (qpos // CHUNK)
    diag = jnp.where(visible[None], table(0), -jnp.inf)
    return diag, table(-T)


def kernel(x, norm_mix_g, w_in, diff_lambda_q1, diff_lambda_k1, diff_lambda_q2, diff_lambda_k2, diff_subln_g,
           rel_bias, w_branch_diff, w_branch_sb, w_out, norm_ffn_g, w_ffn_up, ffn_conv_w, ffn_conv_b, w_ffn_down,
           norm_final_g):
    B, S, D = x.shape
    depth = w_in.shape[0]
    T = ATT_TILE
    n_heads_diff = rel_bias.shape[1]
    qk_w = n_heads_diff * 2 * HEAD_DIM
    ngd = qk_w // LANES
    assert S % T == 0 and T % CHUNK == 0 and T >= MAX_DISTANCE

    near_diag, near_prev = _near_bias_tables(rel_bias)
    upper = (jnp.arange(T)[None, :] > jnp.arange(T)[:, None]).astype(BF16)
    uu = jnp.concatenate([upper, upper], axis=1)

    for layer in range(depth):
        lambda_init = 0.8 - 0.6 * math.exp(-0.3 * layer)
        w = w_in[layer]
        cols = lambda i: w[:, i * qk_w:(i + 1) * qk_w]
        wk = jnp.concatenate([cols(1), cols(4)], axis=1).astype(BF16)
        wqvt = jnp.concatenate([cols(0), cols(3), cols(2), cols(5)], axis=1).T.astype(BF16)
        wg = w[:, 6 * qk_w:].astype(BF16)

        k, qt, vt = _proj(x, norm_mix_g[layer][None], wk, wqvt, 2 * ngd)
        row = lambda a: a[layer][None].astype(F32)
        y_diff = _diff_attention(qt, k, vt, near_diag, near_prev, row(diff_lambda_q1), row(diff_lambda_k1),
                                 row(diff_lambda_q2), row(diff_lambda_k2), row(diff_subln_g), lambda_init, ngd)
        y_sb = _sb_attention(qt, k, vt, uu, ngd, 1)

        x = _mix(x.reshape(B * S, D), y_diff.reshape(B * S, -1), y_sb.reshape(B * S, -1), norm_mix_g[layer][None],
                 wg, w_branch_diff[layer].astype(BF16), w_branch_sb[layer].astype(BF16), w_out[layer].astype(BF16),
                 tm=512).reshape(B, S, D)
        x = _ffn(x, norm_ffn_g[layer][None], w_ffn_up[layer].astype(BF16), ffn_conv_w[layer], ffn_conv_b[layer][None],
                 w_ffn_down[layer].astype(BF16), norm_final_g[None], tm=256, final_norm=layer == depth - 1)
    return x
```

```python
import functools
import math

import jax
import jax.numpy as jnp
from jax import lax
from jax.experimental import pallas as pl
from jax.experimental.pallas import tpu as pltpu

F32 = jnp.float32
BF16 = jnp.bfloat16

HEAD_DIM = 64
CHUNK = 64
N_BUCKETS = 32
MAX_DISTANCE = 128
NORM_EPS = 1e-6
LOG2E = math.log2(math.e)

LANES = 128
SUBLANES = 8
ATT_TILE = 256
VMEM_LIMIT = 56 * 1024 * 1024


def _dot(a, b):
    return jnp.dot(a, b, preferred_element_type=F32)


def _rms(x, g):
    return x * lax.rsqrt(jnp.mean(x * x, axis=-1, keepdims=True) + NORM_EPS) * g


def _proj_kernel(x_ref, g_ref, wk_ref, wqvt_ref, k_ref, qbd_ref, vt_ref, *, n_groups, scale):
    T = x_ref.shape[1]
    h = _rms(x_ref[0], g_ref[...]).astype(BF16)
    kk = _dot(h, wk_ref[...]).astype(BF16)
    for g in range(n_groups):
        k_ref[0, g] = kk[:, g * LANES:(g + 1) * LANES]
    qv = lax.dot_general(wqvt_ref[...], h, (((1,), (1,)), ((), ())), preferred_element_type=F32)
    nq = n_groups * LANES
    first = lax.broadcasted_iota(jnp.int32, (LANES, T), 0) < HEAD_DIM
    zero = jnp.zeros((LANES, T), BF16)
    for g in range(n_groups):
        qg = (qv[g * LANES:(g + 1) * LANES] * scale).astype(BF16)
        qbd_ref[0, 0, g] = jnp.concatenate([jnp.where(first, qg, zero), jnp.where(first, zero, qg)], axis=1)
        vt_ref[0, 0, g] = qv[nq + g * LANES:nq + (g + 1) * LANES].astype(BF16)


def _proj(x, g, wk, wqvt, n_groups):
    B, S, D = x.shape
    T = ATT_TILE
    nt = S // T
    kern = functools.partial(_proj_kernel, n_groups=n_groups, scale=HEAD_DIM ** -0.5 * LOG2E)
    return pl.pallas_call(
        kern,
        grid=(B, nt),
        in_specs=[
            pl.BlockSpec((1, T, D), lambda b, i: (b, i, 0)),
            pl.BlockSpec((1, D), lambda b, i: (0, 0)),
            pl.BlockSpec(wk.shape, lambda b, i: (0, 0)),
            pl.BlockSpec(wqvt.shape, lambda b, i: (0, 0)),
        ],
        out_specs=[
            pl.BlockSpec((1, n_groups, T, LANES), lambda b, i: (b, 0, i, 0)),
            pl.BlockSpec((1, 1, n_groups, LANES, 2 * T), lambda b, i: (b, i, 0, 0, 0)),
            pl.BlockSpec((1, 1, n_groups, LANES, T), lambda b, i: (b, i, 0, 0, 0)),
        ],
        out_shape=[
            jax.ShapeDtypeStruct((B, n_groups, S, LANES), BF16),
            jax.ShapeDtypeStruct((B, nt, n_groups, LANES, 2 * T), BF16),
            jax.ShapeDtypeStruct((B, nt, n_groups, LANES, T), BF16),
        ],
        compiler_params=pltpu.CompilerParams(
            dimension_semantics=("arbitrary", "arbitrary"), vmem_limit_bytes=VMEM_LIMIT),
        name="proj",
    )(x, g, wk, wqvt)


def _key_tile(k_ref, g, j):
    T = ATT_TILE
    start = j * T if isinstance(j, int) else pl.multiple_of(j * T, T)
    return k_ref[0, g, pl.ds(start, T), :]


def _attention_specs(n_groups, group_block, S):
    T = ATT_TILE
    gb = group_block
    return [
        pl.BlockSpec((1, 1, n_groups, LANES, 2 * T), lambda b, i: (b, i, gb, 0, 0)),
        pl.BlockSpec((1, n_groups, S, LANES), lambda b, i: (b, gb, 0, 0)),
        pl.BlockSpec((1, S // T, n_groups, LANES, T), lambda b, i: (b, 0, gb, 0, 0)),
    ]


def _diff_kernel(lq1_ref, lk1_ref, lq2_ref, lk2_ref, gs_ref, nbd_ref, nbp_ref, qbd_ref, k_ref, vt_ref,
                 y_ref, s_ref, mx_ref, l_ref, ot_ref, *, n_groups, lambda_init):
    T = ATT_TILE
    qi = pl.program_id(1)
    groups = range(n_groups)

    def scores(g, j, nb_ref):
        st = _dot(_key_tile(k_ref, g, j), qbd_ref[0, 0, g])
        if nb_ref is not None:
            nb = nb_ref[g]
            st = jnp.concatenate([st[:, :T] + nb, st[:, T:] + nb], axis=1)
        s_ref[g, j] = st
        return jnp.max(st, axis=0, keepdims=True)

    for g in groups:
        mx_ref[g] = scores(g, qi, nbd_ref)

    @pl.when(qi > 0)
    def _():
        for g in groups:
            mx_ref[g] = jnp.maximum(mx_ref[g], scores(g, qi - 1, nbp_ref))

    def far_body(j, carry):
        for g in groups:
            mx_ref[g] = jnp.maximum(mx_ref[g], scores(g, j, None))
        return carry
    lax.fori_loop(0, jnp.maximum(qi - 1, 0), far_body, 0)

    l_ref[...] = jnp.zeros_like(l_ref)
    ot_ref[...] = jnp.zeros_like(ot_ref)

    def pv_body(j, carry):
        for g in groups:
            p = jnp.exp2(s_ref[g, j] - mx_ref[g])
            l_ref[g] += jnp.sum(p, axis=0, keepdims=True)
            ot_ref[g] += _dot(vt_ref[0, j, g], p.astype(BF16))
        return carry
    lax.fori_loop(0, qi + 1, pv_body, 0)

    lam = (jnp.exp(jnp.sum(lq1_ref[...] * lk1_ref[...], keepdims=True))
           - jnp.exp(jnp.sum(lq2_ref[...] * lk2_ref[...], keepdims=True)) + lambda_init)
    for g in groups:
        ot = ot_ref[g] * (1.0 / l_ref[g])
        o = ot[:, :T] - lam * ot[:, T:]
        o = o * lax.rsqrt(jnp.mean(o * o, axis=0, keepdims=True) + NORM_EPS)
        y = o.T * gs_ref[...] * (1.0 - lambda_init)
        y_ref[0, :, g * LANES:(g + 1) * LANES] = y.astype(y_ref.dtype)


def _diff_attention(qbd, k, vt, near_diag, near_prev, lq1, lk1, lq2, lk2, gs, lambda_init, n_groups, group_block):
    B, _, S, _ = k.shape
    T = ATT_TILE
    nt = S // T
    kern = functools.partial(_diff_kernel, n_groups=n_groups, lambda_init=lambda_init)
    small = lambda a: pl.BlockSpec(a.shape, lambda b, i: (0,) * a.ndim)
    return pl.pallas_call(
        kern,
        grid=(B, nt),
        in_specs=[small(lq1), small(lk1), small(lq2), small(lk2), small(gs), small(near_diag), small(near_prev)]
        + _attention_specs(n_groups, group_block, S),
        out_specs=pl.BlockSpec((1, T, n_groups * LANES), lambda b, i: (b, i, 0)),
        out_shape=jax.ShapeDtypeStruct((B, S, n_groups * LANES), BF16),
        scratch_shapes=[
            pltpu.VMEM((n_groups, nt, T, 2 * T), F32),
            pltpu.VMEM((n_groups, 1, 2 * T), F32),
            pltpu.VMEM((n_groups, 1, 2 * T), F32),
            pltpu.VMEM((n_groups, LANES, 2 * T), F32),
        ],
        compiler_params=pltpu.CompilerParams(
            dimension_semantics=("arbitrary", "arbitrary"), vmem_limit_bytes=VMEM_LIMIT),
        name="diff_attn",
    )(lq1, lk1, lq2, lk2, gs, near_diag, near_prev, qbd, k, vt)


def _sb_kernel(nu_ref, qbd_ref, k_ref, vt_ref, y_ref, z_ref, c_ref, ot_ref, *, n_groups):
    T = ATT_TILE
    qi = pl.program_id(1)
    groups = range(n_groups)
    kk = lax.broadcasted_iota(jnp.int32, (T, 2 * T), 0)
    qq = lax.broadcasted_iota(jnp.int32, (T, 2 * T), 1)
    causal = kk < jnp.where(qq >= T, qq - T, qq)
    sign = jnp.uint32(0x80000000)

    c_ref[...] = jnp.zeros_like(c_ref)
    ot_ref[...] = jnp.zeros_like(ot_ref)

    def score_tile(j, slot):
        for g in groups:
            z_ref[slot, g] = _dot(_key_tile(k_ref, g, j), qbd_ref[0, 0, g])

    def tile(j, slot, masked, prefetch=True):
        sps = []
        for g in groups:
            z = z_ref[slot, g]
            neg_abs = lax.bitcast_convert_type(lax.bitcast_convert_type(z, jnp.uint32) | sign, F32)
            sp = jnp.maximum(z, 0.0) + jnp.log2(1.0 + jnp.exp2(neg_abs))
            if masked:
                sp = jnp.where(causal, sp, 0.0)
            sps.append(sp.astype(BF16))
        incls = [_dot(nu_ref[...], sp) for sp in sps]
        if prefetch:
            score_tile(jnp.maximum(j - 1, 0), 1 - slot)
        ws = []
        for g in groups:
            w = jnp.exp2((z_ref[slot, g] + c_ref[g]) + incls[g])
            if masked:
                w = jnp.where(causal, w, 0.0)
            ws.append(w.astype(BF16))
            c_ref[g] += incls[g][0:1]
        for g in groups:
            ot_ref[g] += _dot(vt_ref[0, j, g], ws[g])

    score_tile(qi, 0)
    tile(qi, 0, True)

    def pair(t, carry):
        j = qi - 1 - 2 * t
        tile(j, 1, False)
        tile(j - 1, 0, False)
        return carry
    lax.fori_loop(0, lax.shift_right_logical(qi, 1), pair, 0)

    @pl.when(jnp.bitwise_and(qi, 1) == 1)
    def _():
        tile(0, 1, False, prefetch=False)

    for g in groups:
        ot = ot_ref[g]
        o = jnp.concatenate([ot[:HEAD_DIM, :T], ot[HEAD_DIM:, T:]], axis=0)
        y_ref[0, :, g * LANES:(g + 1) * LANES] = o.T.astype(y_ref.dtype)


def _sb_attention(qbd, k, vt, nu, n_groups, group_block):
    B, _, S, _ = k.shape
    T = ATT_TILE
    kern = functools.partial(_sb_kernel, n_groups=n_groups)
    return pl.pallas_call(
        kern,
        grid=(B, S // T),
        in_specs=[pl.BlockSpec(nu.shape, lambda b, i: (0, 0))] + _attention_specs(n_groups, group_block, S),
        out_specs=pl.BlockSpec((1, T, n_groups * LANES), lambda b, i: (b, i, 0)),
        out_shape=jax.ShapeDtypeStruct((B, S, n_groups * LANES), BF16),
        scratch_shapes=[
            pltpu.VMEM((2, n_groups, T, 2 * T), F32),
            pltpu.VMEM((n_groups, 1, 2 * T), F32),
            pltpu.VMEM((n_groups, LANES, 2 * T), F32),
        ],
        compiler_params=pltpu.CompilerParams(
            dimension_semantics=("arbitrary", "arbitrary"), vmem_limit_bytes=VMEM_LIMIT),
        name="sb_attn",
    )(nu, qbd, k, vt)


def _mix_kernel(x_ref, yd_ref, ys_ref, g_ref, wg_ref, wbd_ref, wbs_ref, wo_ref, o_ref):
    x = x_ref[...]
    D = x.shape[-1]
    h = _rms(x, g_ref[...]).astype(BF16)
    gates = jax.nn.sigmoid(_dot(h, wg_ref[...]))
    merged = gates[:, :D] * _dot(yd_ref[...], wbd_ref[...]) + gates[:, D:] * _dot(ys_ref[...], wbs_ref[...])
    o_ref[...] = x + _dot(merged.astype(BF16), wo_ref[...])


def _mix(x2d, yd, ys, g, wg, wbd, wbs, wo, tm):
    N, D = x2d.shape
    const = lambda a: pl.BlockSpec(a.shape, lambda i: (0, 0))
    row = lambda w: pl.BlockSpec((tm, w), lambda i: (i, 0))
    return pl.pallas_call(
        _mix_kernel,
        grid=(N // tm,),
        in_specs=[row(D), row(yd.shape[1]), row(ys.shape[1]), const(g), const(wg), const(wbd), const(wbs), const(wo)],
        out_specs=row(D),
        out_shape=jax.ShapeDtypeStruct((N, D), F32),
        compiler_params=pltpu.CompilerParams(dimension_semantics=("arbitrary",), vmem_limit_bytes=VMEM_LIMIT),
        name="mix",
    )(x2d, yd, ys, g, wg, wbd, wbs, wo)


def _ffn_kernel(x_ref, g_ref, wup_ref, cw_ref, cb_ref, wdn_ref, gf_ref, o_ref, abuf_ref, *, final_norm):
    tm = x_ref.shape[1]
    F = cw_ref.shape[1]
    H = SUBLANES

    @pl.when(pl.program_id(1) == 0)
    def _():
        abuf_ref[0:H] = jnp.zeros((H, F), F32)

    x = x_ref[0]
    h = _rms(x, g_ref[...]).astype(BF16)
    up = _dot(h, wup_ref[...])
    a = up[:, :F]
    abuf_ref[H:H + tm] = a
    conv = (cw_ref[0:1] * abuf_ref[H - 2:H - 2 + tm] + cw_ref[1:2] * abuf_ref[H - 1:H - 1 + tm]
            + cw_ref[2:3] * a + cb_ref[...])
    abuf_ref[0:H] = abuf_ref[tm:tm + H]
    gelu = 0.5 * conv * (1.0 + lax.erf(conv * (2.0 ** -0.5)))
    act = gelu * up[:, F:]
    x = x + _dot(act.astype(BF16), wdn_ref[...])
    o_ref[0] = _rms(x, gf_ref[...]) if final_norm else x


def _ffn(x, g, wup, cw, cb, wdn, gf, tm, final_norm):
    B, S, D = x.shape
    F = cw.shape[1]
    const = lambda a: pl.BlockSpec(a.shape, lambda b, i: (0, 0), pipeline_mode=pl.Buffered(1))
    return pl.pallas_call(
        functools.partial(_ffn_kernel, final_norm=final_norm),
        grid=(B, S // tm),
        in_specs=[pl.BlockSpec((1, tm, D), lambda b, i: (b, i, 0)),
                  const(g), const(wup), const(cw), const(cb), const(wdn), const(gf)],
        out_specs=pl.BlockSpec((1, tm, D), lambda b, i: (b, i, 0)),
        out_shape=jax.ShapeDtypeStruct((B, S, D), F32),
        scratch_shapes=[pltpu.VMEM((tm + SUBLANES, F), F32)],
        compiler_params=pltpu.CompilerParams(
            dimension_semantics=("arbitrary", "arbitrary"), vmem_limit_bytes=VMEM_LIMIT),
        name="ffn",
    )(x, g, wup, cw, cb, wdn, gf)


def _t5_bucket(rel):
    half = N_BUCKETS // 2
    ret = jnp.where(rel > 0, half, 0)
    n = jnp.abs(rel)
    max_exact = half // 2
    nf = jnp.maximum(n, 1).astype(jnp.float32)
    large = max_exact + (jnp.log(nf / max_exact) / math.log(MAX_DISTANCE / max_exact)
                         * (half - max_exact)).astype(jnp.int32)
    large = jnp.minimum(large, half - 1)
    return ret + jnp.where(n < max_exact, n, large)


def _near_bias_tables(rel_bias):
    T = ATT_TILE
    L = 2 * T
    H = rel_bias.shape[1]
    far = rel_bias[_t5_bucket(jnp.asarray(-MAX_DISTANCE))].astype(F32)
    m = jnp.arange(L)
    rel = jnp.where(m < T, -m, L - m)

    def toeplitz(rels):
        v = ((rel_bias[_t5_bucket(rels)].astype(F32) - far) * LOG2E).T
        rows = jnp.tile(v, (1, T))[:, :T * (L - 1)].reshape(H, T, L - 1)
        return rows[:, :, :T]

    kpos = jnp.arange(T)[:, None]
    qpos = jnp.arange(T)[None, :]
    visible = (kpos // CHUNK) <= (qpos // CHUNK)
    return jnp.where(visible[None], toeplitz(rel), -jnp.inf), toeplitz(rel - T)


def kernel(x, norm_mix_g, w_in, diff_lambda_q1, diff_lambda_k1, diff_lambda_q2, diff_lambda_k2, diff_subln_g,
           rel_bias, w_branch_diff, w_branch_sb, w_out, norm_ffn_g, w_ffn_up, ffn_conv_w, ffn_conv_b, w_ffn_down,
           norm_final_g):
    B, S, D = x.shape
    depth = w_in.shape[0]
    T = ATT_TILE
    n_heads_diff = rel_bias.shape[1]
    qk_w = n_heads_diff * 2 * HEAD_DIM
    ngd = qk_w // LANES
    assert S % T == 0 and T % CHUNK == 0 and T >= MAX_DISTANCE

    near_diag, near_prev = _near_bias_tables(rel_bias)
    neg_upper = -(jnp.arange(T)[None, :] >= jnp.arange(T)[:, None]).astype(BF16)

    for layer in range(depth):
        lambda_init = 0.8 - 0.6 * math.exp(-0.3 * layer)
        w = w_in[layer]
        cols = lambda i: w[:, i * qk_w:(i + 1) * qk_w]
        wk = jnp.concatenate([cols(1), cols(4)], axis=1).astype(BF16)
        wqvt = jnp.concatenate([cols(0), cols(3), cols(2), cols(5)], axis=1).T.astype(BF16)
        wg = w[:, 6 * qk_w:].astype(BF16)

        k, qbd, vt = _proj(x, norm_mix_g[layer][None], wk, wqvt, 2 * ngd)
        row = lambda a: a[layer][None].astype(F32)
        y_diff = _diff_attention(qbd, k, vt, near_diag, near_prev, row(diff_lambda_q1), row(diff_lambda_k1),
                                 row(diff_lambda_q2), row(diff_lambda_k2), row(diff_subln_g), lambda_init, ngd, 0)
        y_sb = _sb_attention(qbd, k, vt, neg_upper, ngd, 1)

        x = _mix(x.reshape(B * S, D), y_diff.reshape(B * S, -1), y_sb.reshape(B * S, -1), norm_mix_g[layer][None],
                 wg, w_branch_diff[layer].astype(BF16), w_branch_sb[layer].astype(BF16), w_out[layer].astype(BF16),
                 tm=512).reshape(B, S, D)
        x = _ffn(x, norm_ffn_g[layer][None], w_ffn_up[layer].astype(BF16), ffn_conv_w[layer], ffn_conv_b[layer][None],
                 w_ffn_down[layer].astype(BF16), norm_final_g[None], tm=256, final_norm=layer == depth - 1)
    return x
```

```python
import functools
import math

import jax
import jax.numpy as jnp
from jax import lax
from jax.experimental import pallas as pl
from jax.experimental.pallas import tpu as pltpu

F32 = jnp.float32
BF16 = jnp.bfloat16

HEAD_DIM = 64
CHUNK = 64
N_BUCKETS = 32
MAX_DISTANCE = 128
NORM_EPS = 1e-6
LOG2E = math.log2(math.e)
DEAD_LOG2 = -150.0

LANES = 128
SUBLANES = 8
ATT_TILE = 256
VMEM_LIMIT = 56 * 1024 * 1024


def _dot(a, b):
    return jnp.dot(a, b, preferred_element_type=F32)


def _rms(x, g):
    return x * lax.rsqrt(jnp.mean(x * x, axis=-1, keepdims=True) + NORM_EPS) * g


def _proj_kernel(x_ref, g_ref, wk_ref, wqvt_ref, k_ref, qbd_ref, vt_ref, *, n_groups, scale):
    T = x_ref.shape[1]
    h = _rms(x_ref[0], g_ref[...]).astype(BF16)
    kk = _dot(h, wk_ref[...]).astype(BF16)
    for g in range(n_groups):
        k_ref[0, g] = kk[:, g * LANES:(g + 1) * LANES]
    qv = lax.dot_general(wqvt_ref[...], h, (((1,), (1,)), ((), ())), preferred_element_type=F32)
    nq = n_groups * LANES
    first = lax.broadcasted_iota(jnp.int32, (LANES, T), 0) < HEAD_DIM
    zero = jnp.zeros((LANES, T), BF16)
    for g in range(n_groups):
        qg = (qv[g * LANES:(g + 1) * LANES] * scale).astype(BF16)
        qbd_ref[0, 0, g] = jnp.concatenate([jnp.where(first, qg, zero), jnp.where(first, zero, qg)], axis=1)
        vt_ref[0, 0, g] = qv[nq + g * LANES:nq + (g + 1) * LANES].astype(BF16)


def _proj(x, g, wk, wqvt, n_groups):
    B, S, D = x.shape
    T = ATT_TILE
    nt = S // T
    kern = functools.partial(_proj_kernel, n_groups=n_groups, scale=HEAD_DIM ** -0.5 * LOG2E)
    return pl.pallas_call(
        kern,
        grid=(B, nt),
        in_specs=[
            pl.BlockSpec((1, T, D), lambda b, i: (b, i, 0)),
            pl.BlockSpec((1, D), lambda b, i: (0, 0)),
            pl.BlockSpec(wk.shape, lambda b, i: (0, 0)),
            pl.BlockSpec(wqvt.shape, lambda b, i: (0, 0)),
        ],
        out_specs=[
            pl.BlockSpec((1, n_groups, T, LANES), lambda b, i: (b, 0, i, 0)),
            pl.BlockSpec((1, 1, n_groups, LANES, 2 * T), lambda b, i: (b, i, 0, 0, 0)),
            pl.BlockSpec((1, 1, n_groups, LANES, T), lambda b, i: (b, i, 0, 0, 0)),
        ],
        out_shape=[
            jax.ShapeDtypeStruct((B, n_groups, S, LANES), BF16),
            jax.ShapeDtypeStruct((B, nt, n_groups, LANES, 2 * T), BF16),
            jax.ShapeDtypeStruct((B, nt, n_groups, LANES, T), BF16),
        ],
        compiler_params=pltpu.CompilerParams(
            dimension_semantics=("arbitrary", "arbitrary"), vmem_limit_bytes=VMEM_LIMIT),
        name="proj",
    )(x, g, wk, wqvt)


def _key_tile(k_ref, g, j):
    T = ATT_TILE
    start = j * T if isinstance(j, int) else pl.multiple_of(j * T, T)
    return k_ref[0, g, pl.ds(start, T), :]


def _attention_specs(n_groups, group_block, S):
    T = ATT_TILE
    gb = group_block
    return [
        pl.BlockSpec((1, 1, n_groups, LANES, 2 * T), lambda b, i: (b, i, gb, 0, 0)),
        pl.BlockSpec((1, n_groups, S, LANES), lambda b, i: (b, gb, 0, 0)),
        pl.BlockSpec((1, S // T, n_groups, LANES, T), lambda b, i: (b, 0, gb, 0, 0)),
    ]


def _diff_kernel(lq1_ref, lk1_ref, lq2_ref, lk2_ref, gs_ref, nbd_ref, nbp_ref, qbd_ref, k_ref, vt_ref,
                 y_ref, z_ref, mx_ref, l_ref, ot_ref, *, n_groups, lambda_init):
    T = ATT_TILE
    qi = pl.program_id(1)
    groups = range(n_groups)

    def score_tile(j, slot, nb_ref):
        for g in groups:
            st = _dot(_key_tile(k_ref, g, j), qbd_ref[0, 0, g])
            if nb_ref is not None:
                nb = nb_ref[g]
                st = jnp.concatenate([st[:, :T] + nb, st[:, T:] + nb], axis=1)
            z_ref[slot, g] = st

    def tile(j, slot, first=False, next_bias=None, prefetch=True):
        ps, alphas = [], []
        for g in groups:
            s = z_ref[slot, g]
            m = jnp.max(s, axis=0, keepdims=True)
            if not first:
                m_old = mx_ref[g]
                m = jnp.maximum(m_old, m)
                alphas.append(jnp.exp2(m_old - m))
            p = jnp.exp2(s - m)
            lsum = jnp.sum(p, axis=0, keepdims=True)
            l_ref[g] = lsum if first else alphas[g] * l_ref[g] + lsum
            mx_ref[g] = m
            ps.append(p.astype(BF16))
        if prefetch:
            score_tile(jnp.maximum(j - 1, 0), 1 - slot, next_bias)
        for g in groups:
            pv = _dot(vt_ref[0, j, g], ps[g])
            ot_ref[g] = pv if first else alphas[g] * ot_ref[g] + pv

    score_tile(qi, 0, nbd_ref)
    tile(qi, 0, first=True, next_bias=nbp_ref)

    @pl.when(qi > 0)
    def _():
        tile(qi - 1, 1)

    rest = jnp.maximum(qi - 1, 0)

    def pair(t, carry):
        j = qi - 2 - 2 * t
        tile(j, 0)
        tile(j - 1, 1)
        return carry
    lax.fori_loop(0, lax.shift_right_logical(rest, 1), pair, 0)

    @pl.when(jnp.bitwise_and(rest, 1) == 1)
    def _():
        tile(0, 0, prefetch=False)

    lam = (jnp.exp(jnp.sum(lq1_ref[...] * lk1_ref[...], keepdims=True))
           - jnp.exp(jnp.sum(lq2_ref[...] * lk2_ref[...], keepdims=True)) + lambda_init)
    for g in groups:
        ot = ot_ref[g] * (1.0 / l_ref[g])
        o = ot[:, :T] - lam * ot[:, T:]
        o = o * lax.rsqrt(jnp.mean(o * o, axis=0, keepdims=True) + NORM_EPS)
        y = o.T * gs_ref[...] * (1.0 - lambda_init)
        y_ref[0, :, g * LANES:(g + 1) * LANES] = y.astype(y_ref.dtype)


def _diff_attention(qbd, k, vt, near_diag, near_prev, lq1, lk1, lq2, lk2, gs, lambda_init, n_groups, group_block):
    B, _, S, _ = k.shape
    T = ATT_TILE
    nt = S // T
    kern = functools.partial(_diff_kernel, n_groups=n_groups, lambda_init=lambda_init)
    small = lambda a: pl.BlockSpec(a.shape, lambda b, i: (0,) * a.ndim)
    return pl.pallas_call(
        kern,
        grid=(B, nt),
        in_specs=[small(lq1), small(lk1), small(lq2), small(lk2), small(gs), small(near_diag), small(near_prev)]
        + _attention_specs(n_groups, group_block, S),
        out_specs=pl.BlockSpec((1, T, n_groups * LANES), lambda b, i: (b, i, 0)),
        out_shape=jax.ShapeDtypeStruct((B, S, n_groups * LANES), BF16),
        scratch_shapes=[
            pltpu.VMEM((2, n_groups, T, 2 * T), F32),
            pltpu.VMEM((n_groups, 1, 2 * T), F32),
            pltpu.VMEM((n_groups, 1, 2 * T), F32),
            pltpu.VMEM((n_groups, LANES, 2 * T), F32),
        ],
        compiler_params=pltpu.CompilerParams(
            dimension_semantics=("arbitrary", "arbitrary"), vmem_limit_bytes=VMEM_LIMIT),
        name="diff_attn",
    )(lq1, lk1, lq2, lk2, gs, near_diag, near_prev, qbd, k, vt)


def _sb_kernel(nu_ref, qbd_ref, k_ref, vt_ref, y_ref, z_ref, c_ref, ot_ref, *, n_groups):
    T = ATT_TILE
    qi = pl.program_id(1)
    groups = range(n_groups)
    kk = lax.broadcasted_iota(jnp.int32, (T, 2 * T), 0)
    qq = lax.broadcasted_iota(jnp.int32, (T, 2 * T), 1)
    causal = kk < jnp.where(qq >= T, qq - T, qq)
    sign = jnp.uint32(0x80000000)

    c_ref[...] = jnp.zeros_like(c_ref)
    ot_ref[...] = jnp.zeros_like(ot_ref)

    def score_tile(j, slot):
        for g in groups:
            z_ref[slot, g] = _dot(_key_tile(k_ref, g, j), qbd_ref[0, 0, g])

    def tile(j, slot, masked, prefetch=True):
        sps = []
        for g in groups:
            z = z_ref[slot, g]
            neg_abs = lax.bitcast_convert_type(lax.bitcast_convert_type(z, jnp.uint32) | sign, F32)
            sp = jnp.maximum(z, 0.0) + jnp.log2(1.0 + jnp.exp2(neg_abs))
            if masked:
                sp = jnp.where(causal, sp, 0.0)
            sps.append(sp.astype(BF16))
        incls = [_dot(nu_ref[...], sp) for sp in sps]
        if prefetch:
            score_tile(jnp.maximum(j - 1, 0), 1 - slot)
        ws = []
        for g in groups:
            w = jnp.exp2((z_ref[slot, g] + c_ref[g]) + incls[g])
            if masked:
                w = jnp.where(causal, w, 0.0)
            ws.append(w.astype(BF16))
            c_ref[g] += incls[g][0:1]
        for g in groups:
            ot_ref[g] += _dot(vt_ref[0, j, g], ws[g])

    score_tile(qi, 0)
    tile(qi, 0, True)

    @pl.when(qi > 0)
    def _():
        tile(qi - 1, 1, False)

    def alive():
        return jnp.max(c_ref[...]) >= DEAD_LOG2

    rest = jnp.maximum(qi - 1, 0)
    n_pairs = lax.shift_right_logical(rest, 1)

    def pair(carry):
        t, _ = carry
        j = qi - 2 - 2 * t
        tile(j, 0, False)
        tile(j - 1, 1, False)
        return t + 1, alive()
    _, go = lax.while_loop(lambda carry: jnp.logical_and(carry[0] < n_pairs, carry[1]), pair,
                           (jnp.int32(0), alive()))

    @pl.when(jnp.logical_and(jnp.bitwise_and(rest, 1) == 1, go))
    def _():
        tile(0, 0, False, prefetch=False)

    for g in groups:
        ot = ot_ref[g]
        o = jnp.concatenate([ot[:HEAD_DIM, :T], ot[HEAD_DIM:, T:]], axis=0)
        y_ref[0, :, g * LANES:(g + 1) * LANES] = o.T.astype(y_ref.dtype)


def _sb_attention(qbd, k, vt, nu, n_groups, group_block):
    B, _, S, _ = k.shape
    T = ATT_TILE
    kern = functools.partial(_sb_kernel, n_groups=n_groups)
    return pl.pallas_call(
        kern,
        grid=(B, S // T),
        in_specs=[pl.BlockSpec(nu.shape, lambda b, i: (0, 0))] + _attention_specs(n_groups, group_block, S),
        out_specs=pl.BlockSpec((1, T, n_groups * LANES), lambda b, i: (b, i, 0)),
        out_shape=jax.ShapeDtypeStruct((B, S, n_groups * LANES), BF16),
        scratch_shapes=[
            pltpu.VMEM((2, n_groups, T, 2 * T), F32),
            pltpu.VMEM((n_groups, 1, 2 * T), F32),
            pltpu.VMEM((n_groups, LANES, 2 * T), F32),
        ],
        compiler_params=pltpu.CompilerParams(
            dimension_semantics=("arbitrary", "arbitrary"), vmem_limit_bytes=VMEM_LIMIT),
        name="sb_attn",
    )(nu, qbd, k, vt)


def _mix_kernel(x_ref, yd_ref, ys_ref, g_ref, wg_ref, wbd_ref, wbs_ref, wo_ref, o_ref):
    x = x_ref[...]
    D = x.shape[-1]
    h = _rms(x, g_ref[...]).astype(BF16)
    gates = jax.nn.sigmoid(_dot(h, wg_ref[...]))
    merged = gates[:, :D] * _dot(yd_ref[...], wbd_ref[...]) + gates[:, D:] * _dot(ys_ref[...], wbs_ref[...])
    o_ref[...] = x + _dot(merged.astype(BF16), wo_ref[...])


def _mix(x2d, yd, ys, g, wg, wbd, wbs, wo, tm):
    N, D = x2d.shape
    const = lambda a: pl.BlockSpec(a.shape, lambda i: (0, 0))
    row = lambda w: pl.BlockSpec((tm, w), lambda i: (i, 0))
    return pl.pallas_call(
        _mix_kernel,
        grid=(N // tm,),
        in_specs=[row(D), row(yd.shape[1]), row(ys.shape[1]), const(g), const(wg), const(wbd), const(wbs), const(wo)],
        out_specs=row(D),
        out_shape=jax.ShapeDtypeStruct((N, D), F32),
        compiler_params=pltpu.CompilerParams(dimension_semantics=("arbitrary",), vmem_limit_bytes=VMEM_LIMIT),
        name="mix",
    )(x2d, yd, ys, g, wg, wbd, wbs, wo)


def _ffn_kernel(x_ref, g_ref, wup_ref, cw_ref, cb_ref, wdn_ref, gf_ref, o_ref, abuf_ref, *, final_norm):
    tm = x_ref.shape[1]
    F = cw_ref.shape[1]
    H = SUBLANES

    @pl.when(pl.program_id(1) == 0)
    def _():
        abuf_ref[0:H] = jnp.zeros((H, F), F32)

    x = x_ref[0]
    h = _rms(x, g_ref[...]).astype(BF16)
    up = _dot(h, wup_ref[...])
    a = up[:, :F]
    abuf_ref[H:H + tm] = a
    conv = (cw_ref[0:1] * abuf_ref[H - 2:H - 2 + tm] + cw_ref[1:2] * abuf_ref[H - 1:H - 1 + tm]
            + cw_ref[2:3] * a + cb_ref[...])
    abuf_ref[0:H] = abuf_ref[tm:tm + H]
    gelu = 0.5 * conv * (1.0 + lax.erf(conv * (2.0 ** -0.5)))
    act = gelu * up[:, F:]
    x = x + _dot(act.astype(BF16), wdn_ref[...])
    o_ref[0] = _rms(x, gf_ref[...]) if final_norm else x


def _ffn(x, g, wup, cw, cb, wdn, gf, tm, final_norm):
    B, S, D = x.shape
    F = cw.shape[1]
    const = lambda a: pl.BlockSpec(a.shape, lambda b, i: (0, 0), pipeline_mode=pl.Buffered(1))
    return pl.pallas_call(
        functools.partial(_ffn_kernel, final_norm=final_norm),
        grid=(B, S // tm),
        in_specs=[pl.BlockSpec((1, tm, D), lambda b, i: (b, i, 0)),
                  const(g), const(wup), const(cw), const(cb), const(wdn), const(gf)],
        out_specs=pl.BlockSpec((1, tm, D), lambda b, i: (b, i, 0)),
        out_shape=jax.ShapeDtypeStruct((B, S, D), F32),
        scratch_shapes=[pltpu.VMEM((tm + SUBLANES, F), F32)],
        compiler_params=pltpu.CompilerParams(
            dimension_semantics=("arbitrary", "arbitrary"), vmem_limit_bytes=VMEM_LIMIT),
        name="ffn",
    )(x, g, wup, cw, cb, wdn, gf)


def _t5_bucket(rel):
    half = N_BUCKETS // 2
    ret = jnp.where(rel > 0, half, 0)
    n = jnp.abs(rel)
    max_exact = half // 2
    nf = jnp.maximum(n, 1).astype(jnp.float32)
    large = max_exact + (jnp.log(nf / max_exact) / math.log(MAX_DISTANCE / max_exact)
                         * (half - max_exact)).astype(jnp.int32)
    large = jnp.minimum(large, half - 1)
    return ret + jnp.where(n < max_exact, n, large)


def _near_bias_tables(rel_bias):
    T = ATT_TILE
    L = 2 * T
    H = rel_bias.shape[1]
    far = rel_bias[_t5_bucket(jnp.asarray(-MAX_DISTANCE))].astype(F32)
    m = jnp.arange(L)
    rel = jnp.where(m < T, -m, L - m)

    def toeplitz(rels):
        v = ((rel_bias[_t5_bucket(rels)].astype(F32) - far) * LOG2E).T
        rows = jnp.tile(v, (1, T))[:, :T * (L - 1)].reshape(H, T, L - 1)
        return rows[:, :, :T]

    kpos = jnp.arange(T)[:, None]
    qpos = jnp.arange(T)[None, :]
    visible = (kpos // CHUNK) <= (qpos // CHUNK)
    return jnp.where(visible[None], toeplitz(rel), -jnp.inf), toeplitz(rel - T)


def kernel(x, norm_mix_g, w_in, diff_lambda_q1, diff_lambda_k1, diff_lambda_q2, diff_lambda_k2, diff_subln_g,
           rel_bias, w_branch_diff, w_branch_sb, w_out, norm_ffn_g, w_ffn_up, ffn_conv_w, ffn_conv_b, w_ffn_down,
           norm_final_g):
    B, S, D = x.shape
    depth = w_in.shape[0]
    T = ATT_TILE
    n_heads_diff = rel_bias.shape[1]
    qk_w = n_heads_diff * 2 * HEAD_DIM
    ngd = qk_w // LANES
    assert S % T == 0 and T % CHUNK == 0 and T >= MAX_DISTANCE

    near_diag, near_prev = _near_bias_tables(rel_bias)
    neg_upper = -(jnp.arange(T)[None, :] >= jnp.arange(T)[:, None]).astype(BF16)

    for layer in range(depth):
        lambda_init = 0.8 - 0.6 * math.exp(-0.3 * layer)
        w = w_in[layer]
        cols = lambda i: w[:, i * qk_w:(i + 1) * qk_w]
        wk = jnp.concatenate([cols(1), cols(4)], axis=1).astype(BF16)
        wqvt = jnp.concatenate([cols(0), cols(3), cols(2), cols(5)], axis=1).T.astype(BF16)
        wg = w[:, 6 * qk_w:].astype(BF16)

        k, qbd, vt = _proj(x, norm_mix_g[layer][None], wk, wqvt, 2 * ngd)
        row = lambda a: a[layer][None].astype(F32)
        y_diff = _diff_attention(qbd, k, vt, near_diag, near_prev, row(diff_lambda_q1), row(diff_lambda_k1),
                                 row(diff_lambda_q2), row(diff_lambda_k2), row(diff_subln_g), lambda_init, ngd, 0)
        y_sb = _sb_attention(qbd, k, vt, neg_upper, ngd, 1)

        x = _mix(x.reshape(B * S, D), y_diff.reshape(B * S, -1), y_sb.reshape(B * S, -1), norm_mix_g[layer][None],
                 wg, w_branch_diff[layer].astype(BF16), w_branch_sb[layer].astype(BF16), w_out[layer].astype(BF16),
                 tm=512).reshape(B, S, D)
        x = _ffn(x, norm_ffn_g[layer][None], w_ffn_up[layer].astype(BF16), ffn_conv_w[layer], ffn_conv_b[layer][None],
                 w_ffn_down[layer].astype(BF16), norm_final_g[None], tm=256, final_norm=layer == depth - 1)
    return x
```

```python
import functools
import math

import jax
import jax.numpy as jnp
from jax import lax
from jax.experimental import pallas as pl
from jax.experimental.pallas import tpu as pltpu

F32 = jnp.float32
BF16 = jnp.bfloat16

HEAD_DIM = 64
CHUNK = 64
N_BUCKETS = 32
MAX_DISTANCE = 128
NORM_EPS = 1e-6
LOG2E = math.log2(math.e)
DEAD_LOG2 = -150.0

LANES = 128
SUBLANES = 8
ATT_TILE = 256
VMEM_LIMIT = 56 * 1024 * 1024


def _dot(a, b):
    return jnp.dot(a, b, preferred_element_type=F32)


def _rms(x, g):
    return x * lax.rsqrt(jnp.mean(x * x, axis=-1, keepdims=True) + NORM_EPS) * g


def _proj_kernel(x_ref, g_ref, wk_ref, wqvt_ref, k_ref, qbd_ref, vt_ref, *, n_groups, scale):
    T = ATT_TILE
    tiles = x_ref.shape[1] // T
    h = _rms(x_ref[0], g_ref[...]).astype(BF16)
    kk = _dot(h, wk_ref[...]).astype(BF16)
    for g in range(n_groups):
        k_ref[0, g] = kk[:, g * LANES:(g + 1) * LANES]
    qv = lax.dot_general(wqvt_ref[...], h, (((1,), (1,)), ((), ())), preferred_element_type=F32)
    nq = n_groups * LANES
    first = lax.broadcasted_iota(jnp.int32, (LANES, T), 0) < HEAD_DIM
    zero = jnp.zeros((LANES, T), BF16)
    for t in range(tiles):
        for g in range(n_groups):
            qg = (qv[g * LANES:(g + 1) * LANES, t * T:(t + 1) * T] * scale).astype(BF16)
            qbd_ref[0, t, g] = jnp.concatenate([jnp.where(first, qg, zero), jnp.where(first, zero, qg)], axis=1)
            vt_ref[0, t, g] = qv[nq + g * LANES:nq + (g + 1) * LANES, t * T:(t + 1) * T].astype(BF16)


def _proj(x, g, wk, wqvt, n_groups, tiles_per_step):
    B, S, D = x.shape
    T = ATT_TILE
    nt = S // T
    tps = tiles_per_step
    kern = functools.partial(_proj_kernel, n_groups=n_groups, scale=HEAD_DIM ** -0.5 * LOG2E)
    return pl.pallas_call(
        kern,
        grid=(B, nt // tps),
        in_specs=[
            pl.BlockSpec((1, tps * T, D), lambda b, i: (b, i, 0)),
            pl.BlockSpec((1, D), lambda b, i: (0, 0)),
            pl.BlockSpec(wk.shape, lambda b, i: (0, 0)),
            pl.BlockSpec(wqvt.shape, lambda b, i: (0, 0)),
        ],
        out_specs=[
            pl.BlockSpec((1, n_groups, tps * T, LANES), lambda b, i: (b, 0, i, 0)),
            pl.BlockSpec((1, tps, n_groups, LANES, 2 * T), lambda b, i: (b, i, 0, 0, 0)),
            pl.BlockSpec((1, tps, n_groups, LANES, T), lambda b, i: (b, i, 0, 0, 0)),
        ],
        out_shape=[
            jax.ShapeDtypeStruct((B, n_groups, S, LANES), BF16),
            jax.ShapeDtypeStruct((B, nt, n_groups, LANES, 2 * T), BF16),
            jax.ShapeDtypeStruct((B, nt, n_groups, LANES, T), BF16),
        ],
        compiler_params=pltpu.CompilerParams(
            dimension_semantics=("arbitrary", "arbitrary"), vmem_limit_bytes=VMEM_LIMIT),
        name="proj",
    )(x, g, wk, wqvt)


def _key_tile(k_ref, g, j):
    T = ATT_TILE
    start = j * T if isinstance(j, int) else pl.multiple_of(j * T, T)
    return k_ref[0, g, pl.ds(start, T), :]


def _attention_specs(n_groups, group_block, S):
    T = ATT_TILE
    gb = group_block
    return [
        pl.BlockSpec((1, 1, n_groups, LANES, 2 * T), lambda b, i: (b, i, gb, 0, 0)),
        pl.BlockSpec((1, n_groups, S, LANES), lambda b, i: (b, gb, 0, 0)),
        pl.BlockSpec((1, S // T, n_groups, LANES, T), lambda b, i: (b, 0, gb, 0, 0)),
    ]


def _diff_kernel(lq1_ref, lk1_ref, lq2_ref, lk2_ref, gs_ref, nbd_ref, nbp_ref, qbd_ref, k_ref, vt_ref,
                 y_ref, z_ref, mx_ref, l_ref, ot_ref, *, n_groups, lambda_init):
    T = ATT_TILE
    qi = pl.program_id(1)
    groups = range(n_groups)

    def score_group(g, j, slot, nb_ref):
        st = _dot(_key_tile(k_ref, g, j), qbd_ref[0, 0, g])
        if nb_ref is not None:
            nb = nb_ref[g]
            st = jnp.concatenate([st[:, :T] + nb, st[:, T:] + nb], axis=1)
        z_ref[slot, g] = st

    def tile(j, slot, first=False, next_bias=None, prefetch=True):
        ps, alphas = [], []
        for g in groups:
            s = z_ref[slot, g]
            m = jnp.max(s, axis=0, keepdims=True)
            if not first:
                m_old = mx_ref[g]
                m = jnp.maximum(m_old, m)
                alphas.append(jnp.exp2(m_old - m))
            p = jnp.exp2(s - m)
            lsum = jnp.sum(p, axis=0, keepdims=True)
            l_ref[g] = lsum if first else alphas[g] * l_ref[g] + lsum
            mx_ref[g] = m
            ps.append(p.astype(BF16))
        for g in groups:
            if prefetch:
                score_group(g, jnp.maximum(j - 1, 0), 1 - slot, next_bias)
            pv = _dot(vt_ref[0, j, g], ps[g])
            ot_ref[g] = pv if first else alphas[g] * ot_ref[g] + pv

    for g in groups:
        score_group(g, qi, 0, nbd_ref)
    tile(qi, 0, first=True, next_bias=nbp_ref)

    @pl.when(qi > 0)
    def _():
        tile(qi - 1, 1)

    rest = jnp.maximum(qi - 1, 0)

    def pair(t, carry):
        j = qi - 2 - 2 * t
        tile(j, 0)
        tile(j - 1, 1)
        return carry
    lax.fori_loop(0, lax.shift_right_logical(rest, 1), pair, 0)

    @pl.when(jnp.bitwise_and(rest, 1) == 1)
    def _():
        tile(0, 0, prefetch=False)

    lam = (jnp.exp(jnp.sum(lq1_ref[...] * lk1_ref[...], keepdims=True))
           - jnp.exp(jnp.sum(lq2_ref[...] * lk2_ref[...], keepdims=True)) + lambda_init)
    for g in groups:
        ot = ot_ref[g] * (1.0 / l_ref[g])
        o = ot[:, :T] - lam * ot[:, T:]
        o = o * lax.rsqrt(jnp.mean(o * o, axis=0, keepdims=True) + NORM_EPS)
        y = o.T * gs_ref[...] * (1.0 - lambda_init)
        y_ref[0, :, g * LANES:(g + 1) * LANES] = y.astype(y_ref.dtype)


def _diff_attention(qbd, k, vt, near_diag, near_prev, lq1, lk1, lq2, lk2, gs, lambda_init, n_groups, group_block):
    B, _, S, _ = k.shape
    T = ATT_TILE
    nt = S // T
    kern = functools.partial(_diff_kernel, n_groups=n_groups, lambda_init=lambda_init)
    small = lambda a: pl.BlockSpec(a.shape, lambda b, i: (0,) * a.ndim)
    return pl.pallas_call(
        kern,
        grid=(B, nt),
        in_specs=[small(lq1), small(lk1), small(lq2), small(lk2), small(gs), small(near_diag), small(near_prev)]
        + _attention_specs(n_groups, group_block, S),
        out_specs=pl.BlockSpec((1, T, n_groups * LANES), lambda b, i: (b, i, 0)),
        out_shape=jax.ShapeDtypeStruct((B, S, n_groups * LANES), BF16),
        scratch_shapes=[
            pltpu.VMEM((2, n_groups, T, 2 * T), F32),
            pltpu.VMEM((n_groups, 1, 2 * T), F32),
            pltpu.VMEM((n_groups, 1, 2 * T), F32),
            pltpu.VMEM((n_groups, LANES, 2 * T), F32),
        ],
        compiler_params=pltpu.CompilerParams(
            dimension_semantics=("arbitrary", "arbitrary"), vmem_limit_bytes=VMEM_LIMIT),
        name="diff_attn",
    )(lq1, lk1, lq2, lk2, gs, near_diag, near_prev, qbd, k, vt)


def _sb_kernel(nu_ref, qbd_ref, k_ref, vt_ref, y_ref, z_ref, c_ref, ot_ref, *, n_groups):
    T = ATT_TILE
    qi = pl.program_id(1)
    groups = range(n_groups)
    kk = lax.broadcasted_iota(jnp.int32, (T, 2 * T), 0)
    qq = lax.broadcasted_iota(jnp.int32, (T, 2 * T), 1)
    causal = kk < jnp.where(qq >= T, qq - T, qq)
    sign = jnp.uint32(0x80000000)

    c_ref[...] = jnp.zeros_like(c_ref)
    ot_ref[...] = jnp.zeros_like(ot_ref)

    def score_group(g, j, slot):
        z_ref[slot, g] = _dot(_key_tile(k_ref, g, j), qbd_ref[0, 0, g])

    def tile(j, slot, masked, prefetch=True):
        sps = []
        for g in groups:
            z = z_ref[slot, g]
            neg_abs = lax.bitcast_convert_type(lax.bitcast_convert_type(z, jnp.uint32) | sign, F32)
            sp = jnp.maximum(z, 0.0) + jnp.log2(1.0 + jnp.exp2(neg_abs))
            if masked:
                sp = jnp.where(causal, sp, 0.0)
            sps.append(sp.astype(BF16))
        incls = []
        for g in groups:
            if prefetch:
                score_group(g, jnp.maximum(j - 1, 0), 1 - slot)
            incls.append(_dot(nu_ref[...], sps[g]))
        ws = []
        for g in groups:
            w = jnp.exp2((z_ref[slot, g] + c_ref[g]) + incls[g])
            if masked:
                w = jnp.where(causal, w, 0.0)
            ws.append(w.astype(BF16))
            c_ref[g] += incls[g][0:1]
        for g in groups:
            ot_ref[g] += _dot(vt_ref[0, j, g], ws[g])

    for g in groups:
        score_group(g, qi, 0)
    tile(qi, 0, True)

    @pl.when(qi > 0)
    def _():
        tile(qi - 1, 1, False)

    def alive():
        return jnp.max(c_ref[...]) >= DEAD_LOG2

    rest = jnp.maximum(qi - 1, 0)
    n_pairs = lax.shift_right_logical(rest, 1)

    def pair(carry):
        t, _ = carry
        j = qi - 2 - 2 * t
        tile(j, 0, False)
        tile(j - 1, 1, False)
        return t + 1, alive()
    _, go = lax.while_loop(lambda carry: jnp.logical_and(carry[0] < n_pairs, carry[1]), pair,
                           (jnp.int32(0), alive()))

    @pl.when(jnp.logical_and(jnp.bitwise_and(rest, 1) == 1, go))
    def _():
        tile(0, 0, False, prefetch=False)

    for g in groups:
        ot = ot_ref[g]
        o = jnp.concatenate([ot[:HEAD_DIM, :T], ot[HEAD_DIM:, T:]], axis=0)
        y_ref[0, :, g * LANES:(g + 1) * LANES] = o.T.astype(y_ref.dtype)


def _sb_attention(qbd, k, vt, nu, n_groups, group_block):
    B, _, S, _ = k.shape
    T = ATT_TILE
    kern = functools.partial(_sb_kernel, n_groups=n_groups)
    return pl.pallas_call(
        kern,
        grid=(B, S // T),
        in_specs=[pl.BlockSpec(nu.shape, lambda b, i: (0, 0))] + _attention_specs(n_groups, group_block, S),
        out_specs=pl.BlockSpec((1, T, n_groups * LANES), lambda b, i: (b, i, 0)),
        out_shape=jax.ShapeDtypeStruct((B, S, n_groups * LANES), BF16),
        scratch_shapes=[
            pltpu.VMEM((2, n_groups, T, 2 * T), F32),
            pltpu.VMEM((n_groups, 1, 2 * T), F32),
            pltpu.VMEM((n_groups, LANES, 2 * T), F32),
        ],
        compiler_params=pltpu.CompilerParams(
            dimension_semantics=("arbitrary", "arbitrary"), vmem_limit_bytes=VMEM_LIMIT),
        name="sb_attn",
    )(nu, qbd, k, vt)


def _mix_kernel(x_ref, yd_ref, ys_ref, g_ref, wg_ref, wbd_ref, wbs_ref, wo_ref, o_ref):
    x = x_ref[...]
    D = x.shape[-1]
    h = _rms(x, g_ref[...]).astype(BF16)
    gates = jax.nn.sigmoid(_dot(h, wg_ref[...]))
    merged = gates[:, :D] * _dot(yd_ref[...], wbd_ref[...]) + gates[:, D:] * _dot(ys_ref[...], wbs_ref[...])
    o_ref[...] = x + _dot(merged.astype(BF16), wo_ref[...])


def _mix(x2d, yd, ys, g, wg, wbd, wbs, wo, tm):
    N, D = x2d.shape
    const = lambda a: pl.BlockSpec(a.shape, lambda i: (0, 0))
    row = lambda w: pl.BlockSpec((tm, w), lambda i: (i, 0))
    return pl.pallas_call(
        _mix_kernel,
        grid=(N // tm,),
        in_specs=[row(D), row(yd.shape[1]), row(ys.shape[1]), const(g), const(wg), const(wbd), const(wbs), const(wo)],
        out_specs=row(D),
        out_shape=jax.ShapeDtypeStruct((N, D), F32),
        compiler_params=pltpu.CompilerParams(dimension_semantics=("arbitrary",), vmem_limit_bytes=VMEM_LIMIT),
        name="mix",
    )(x2d, yd, ys, g, wg, wbd, wbs, wo)


def _ffn_kernel(x_ref, g_ref, wup_ref, cw_ref, cb_ref, wdn_ref, gf_ref, o_ref, abuf_ref, *, final_norm):
    tm = x_ref.shape[1]
    F = cw_ref.shape[1]
    H = SUBLANES

    @pl.when(pl.program_id(1) == 0)
    def _():
        abuf_ref[0:H] = jnp.zeros((H, F), F32)

    x = x_ref[0]
    h = _rms(x, g_ref[...]).astype(BF16)
    up = _dot(h, wup_ref[...])
    a = up[:, :F]
    abuf_ref[H:H + tm] = a
    conv = (cw_ref[0:1] * abuf_ref[H - 2:H - 2 + tm] + cw_ref[1:2] * abuf_ref[H - 1:H - 1 + tm]
            + cw_ref[2:3] * a + cb_ref[...])
    abuf_ref[0:H] = abuf_ref[tm:tm + H]
    gelu = 0.5 * conv * (1.0 + lax.erf(conv * (2.0 ** -0.5)))
    act = gelu * up[:, F:]
    x = x + _dot(act.astype(BF16), wdn_ref[...])
    o_ref[0] = _rms(x, gf_ref[...]) if final_norm else x


def _ffn(x, g, wup, cw, cb, wdn, gf, tm, final_norm):
    B, S, D = x.shape
    F = cw.shape[1]
    const = lambda a: pl.BlockSpec(a.shape, lambda b, i: (0, 0), pipeline_mode=pl.Buffered(1))
    return pl.pallas_call(
        functools.partial(_ffn_kernel, final_norm=final_norm),
        grid=(B, S // tm),
        in_specs=[pl.BlockSpec((1, tm, D), lambda b, i: (b, i, 0)),
                  const(g), const(wup), const(cw), const(cb), const(wdn), const(gf)],
        out_specs=pl.BlockSpec((1, tm, D), lambda b, i: (b, i, 0)),
        out_shape=jax.ShapeDtypeStruct((B, S, D), F32),
        scratch_shapes=[pltpu.VMEM((tm + SUBLANES, F), F32)],
        compiler_params=pltpu.CompilerParams(
            dimension_semantics=("arbitrary", "arbitrary"), vmem_limit_bytes=VMEM_LIMIT),
        name="ffn",
    )(x, g, wup, cw, cb, wdn, gf)


def _t5_bucket(rel):
    half = N_BUCKETS // 2
    ret = jnp.where(rel > 0, half, 0)
    n = jnp.abs(rel)
    max_exact = half // 2
    nf = jnp.maximum(n, 1).astype(jnp.float32)
    large = max_exact + (jnp.log(nf / max_exact) / math.log(MAX_DISTANCE / max_exact)
                         * (half - max_exact)).astype(jnp.int32)
    large = jnp.minimum(large, half - 1)
    return ret + jnp.where(n < max_exact, n, large)


def _near_bias_tables(rel_bias):
    T = ATT_TILE
    L = 2 * T
    H = rel_bias.shape[1]
    far = rel_bias[_t5_bucket(jnp.asarray(-MAX_DISTANCE))].astype(F32)
    m = jnp.arange(L)
    rel = jnp.where(m < T, -m, L - m)

    def toeplitz(rels):
        v = ((rel_bias[_t5_bucket(rels)].astype(F32) - far) * LOG2E).T
        rows = jnp.tile(v, (1, T))[:, :T * (L - 1)].reshape(H, T, L - 1)
        return rows[:, :, :T]

    kpos = jnp.arange(T)[:, None]
    qpos = jnp.arange(T)[None, :]
    visible = (kpos // CHUNK) <= (qpos // CHUNK)
    return jnp.where(visible[None], toeplitz(rel), -jnp.inf), toeplitz(rel - T)


def kernel(x, norm_mix_g, w_in, diff_lambda_q1, diff_lambda_k1, diff_lambda_q2, diff_lambda_k2, diff_subln_g,
           rel_bias, w_branch_diff, w_branch_sb, w_out, norm_ffn_g, w_ffn_up, ffn_conv_w, ffn_conv_b, w_ffn_down,
           norm_final_g):
    B, S, D = x.shape
    depth = w_in.shape[0]
    T = ATT_TILE
    n_heads_diff = rel_bias.shape[1]
    qk_w = n_heads_diff * 2 * HEAD_DIM
    ngd = qk_w // LANES
    assert S % T == 0 and T % CHUNK == 0 and T >= MAX_DISTANCE

    near_diag, near_prev = _near_bias_tables(rel_bias)
    neg_upper = -(jnp.arange(T)[None, :] >= jnp.arange(T)[:, None]).astype(BF16)

    for layer in range(depth):
        lambda_init = 0.8 - 0.6 * math.exp(-0.3 * layer)
        w = w_in[layer]
        cols = lambda i: w[:, i * qk_w:(i + 1) * qk_w]
        wk = jnp.concatenate([cols(1), cols(4)], axis=1).astype(BF16)
        wqvt = jnp.concatenate([cols(0), cols(3), cols(2), cols(5)], axis=1).T.astype(BF16)
        wg = w[:, 6 * qk_w:].astype(BF16)

        k, qbd, vt = _proj(x, norm_mix_g[layer][None], wk, wqvt, 2 * ngd, tiles_per_step=2)
        row = lambda a: a[layer][None].astype(F32)
        y_diff = _diff_attention(qbd, k, vt, near_diag, near_prev, row(diff_lambda_q1), row(diff_lambda_k1),
                                 row(diff_lambda_q2), row(diff_lambda_k2), row(diff_subln_g), lambda_init, ngd, 0)
        y_sb = _sb_attention(qbd, k, vt, neg_upper, ngd, 1)

        x = _mix(x.reshape(B * S, D), y_diff.reshape(B * S, -1), y_sb.reshape(B * S, -1), norm_mix_g[layer][None],
                 wg, w_branch_diff[layer].astype(BF16), w_branch_sb[layer].astype(BF16), w_out[layer].astype(BF16),
                 tm=512).reshape(B, S, D)
        x = _ffn(x, norm_ffn_g[layer][None], w_ffn_up[layer].astype(BF16), ffn_conv_w[layer], ffn_conv_b[layer][None],
                 w_ffn_down[layer].astype(BF16), norm_final_g[None], tm=512, final_norm=layer == depth - 1)
    return x
```

```python
import functools
import math

import jax
import jax.numpy as jnp
from jax import lax
from jax.experimental import pallas as pl
from jax.experimental.pallas import tpu as pltpu

F32 = jnp.float32
BF16 = jnp.bfloat16

HEAD_DIM = 64
CHUNK = 64
N_BUCKETS = 32
MAX_DISTANCE = 128
NORM_EPS = 1e-6
LOG2E = math.log2(math.e)
DEAD_LOG2 = -150.0

LANES = 128
SUBLANES = 8
ATT_TILE = 256
BF16_ROWS = 16
VT_ROWS = LANES + BF16_ROWS
VMEM_LIMIT = 56 * 1024 * 1024


def _dot(a, b):
    return jnp.dot(a, b, preferred_element_type=F32)


def _rms(x, g):
    return x * lax.rsqrt(jnp.mean(x * x, axis=-1, keepdims=True) + NORM_EPS) * g


def _proj_kernel(x_ref, g_ref, wk_ref, wqvt_ref, k_ref, qbd_ref, vt_ref, *, n_groups, scale):
    T = ATT_TILE
    tiles = x_ref.shape[1] // T
    h = _rms(x_ref[0], g_ref[...]).astype(BF16)
    kk = _dot(h, wk_ref[...]).astype(BF16)
    for g in range(n_groups):
        k_ref[0, g] = kk[:, g * LANES:(g + 1) * LANES]
    qv = lax.dot_general(wqvt_ref[...], h, (((1,), (1,)), ((), ())), preferred_element_type=F32)
    nq = n_groups * LANES
    first = lax.broadcasted_iota(jnp.int32, (LANES, T), 0) < HEAD_DIM
    zero = jnp.zeros((LANES, T), BF16)
    for t in range(tiles):
        for g in range(n_groups):
            qg = (qv[g * LANES:(g + 1) * LANES, t * T:(t + 1) * T] * scale).astype(BF16)
            qbd_ref[0, t, g] = jnp.concatenate([jnp.where(first, qg, zero), jnp.where(first, zero, qg)], axis=1)
            vt_ref[0, t, g, :LANES] = qv[nq + g * LANES:nq + (g + 1) * LANES, t * T:(t + 1) * T].astype(BF16)
            vt_ref[0, t, g, LANES:] = jnp.ones((BF16_ROWS, T), BF16)


def _proj(x, g, wk, wqvt, n_groups, tiles_per_step):
    B, S, D = x.shape
    T = ATT_TILE
    nt = S // T
    tps = tiles_per_step
    kern = functools.partial(_proj_kernel, n_groups=n_groups, scale=HEAD_DIM ** -0.5 * LOG2E)
    return pl.pallas_call(
        kern,
        grid=(B, nt // tps),
        in_specs=[
            pl.BlockSpec((1, tps * T, D), lambda b, i: (b, i, 0)),
            pl.BlockSpec((1, D), lambda b, i: (0, 0)),
            pl.BlockSpec(wk.shape, lambda b, i: (0, 0)),
            pl.BlockSpec(wqvt.shape, lambda b, i: (0, 0)),
        ],
        out_specs=[
            pl.BlockSpec((1, n_groups, tps * T, LANES), lambda b, i: (b, 0, i, 0)),
            pl.BlockSpec((1, tps, n_groups, LANES, 2 * T), lambda b, i: (b, i, 0, 0, 0)),
            pl.BlockSpec((1, tps, n_groups, VT_ROWS, T), lambda b, i: (b, i, 0, 0, 0)),
        ],
        out_shape=[
            jax.ShapeDtypeStruct((B, n_groups, S, LANES), BF16),
            jax.ShapeDtypeStruct((B, nt, n_groups, LANES, 2 * T), BF16),
            jax.ShapeDtypeStruct((B, nt, n_groups, VT_ROWS, T), BF16),
        ],
        compiler_params=pltpu.CompilerParams(
            dimension_semantics=("arbitrary", "arbitrary"), vmem_limit_bytes=VMEM_LIMIT),
        name="proj",
    )(x, g, wk, wqvt)


def _key_tile(k_ref, g, j):
    T = ATT_TILE
    start = j * T if isinstance(j, int) else pl.multiple_of(j * T, T)
    return k_ref[0, g, pl.ds(start, T), :]


def _attention_specs(n_groups, group_block, S):
    T = ATT_TILE
    gb = group_block
    return [
        pl.BlockSpec((1, 1, n_groups, LANES, 2 * T), lambda b, i: (b, i, gb, 0, 0)),
        pl.BlockSpec((1, n_groups, S, LANES), lambda b, i: (b, gb, 0, 0)),
        pl.BlockSpec((1, S // T, n_groups, VT_ROWS, T), lambda b, i: (b, 0, gb, 0, 0)),
    ]


def _diff_kernel(lq1_ref, lk1_ref, lq2_ref, lk2_ref, gs_ref, nbd_ref, nbp_ref, qbd_ref, k_ref, vt_ref,
                 y_ref, z_ref, mx_ref, ot_ref, *, n_groups, lambda_init):
    T = ATT_TILE
    qi = pl.program_id(1)
    groups = range(n_groups)

    def score_group(g, j, slot, nb_ref):
        st = _dot(_key_tile(k_ref, g, j), qbd_ref[0, 0, g])
        if nb_ref is not None:
            nb = nb_ref[g]
            st = jnp.concatenate([st[:, :T] + nb, st[:, T:] + nb], axis=1)
        z_ref[slot, g] = st

    def tile(j, slot, first=False, next_bias=None, prefetch=True):
        ps, alphas = [], []
        for g in groups:
            s = z_ref[slot, g]
            m = jnp.max(s, axis=0, keepdims=True)
            if not first:
                m_old = mx_ref[g]
                m = jnp.maximum(m_old, m)
                alphas.append(jnp.exp2(m_old - m))
            mx_ref[g] = m
            ps.append(jnp.exp2((s - m).astype(BF16)))
        for g in groups:
            if prefetch:
                score_group(g, jnp.maximum(j - 1, 0), 1 - slot, next_bias)
            pv = _dot(vt_ref[0, j, g], ps[g])
            ot_ref[g] = pv if first else alphas[g] * ot_ref[g] + pv

    for g in groups:
        score_group(g, qi, 0, nbd_ref)
    tile(qi, 0, first=True, next_bias=nbp_ref)

    @pl.when(qi > 0)
    def _():
        tile(qi - 1, 1)

    rest = jnp.maximum(qi - 1, 0)

    def pair(t, carry):
        j = qi - 2 - 2 * t
        tile(j, 0)
        tile(j - 1, 1)
        return carry
    lax.fori_loop(0, lax.shift_right_logical(rest, 1), pair, 0)

    @pl.when(jnp.bitwise_and(rest, 1) == 1)
    def _():
        tile(0, 0, prefetch=False)

    lam = (jnp.exp(jnp.sum(lq1_ref[...] * lk1_ref[...], keepdims=True))
           - jnp.exp(jnp.sum(lq2_ref[...] * lk2_ref[...], keepdims=True)) + lambda_init)
    for g in groups:
        ot = ot_ref[g]
        ot = ot[:LANES] * (1.0 / ot[LANES:LANES + 1])
        o = ot[:, :T] - lam * ot[:, T:]
        o = o * lax.rsqrt(jnp.mean(o * o, axis=0, keepdims=True) + NORM_EPS)
        y = o.T * gs_ref[...] * (1.0 - lambda_init)
        y_ref[0, :, g * LANES:(g + 1) * LANES] = y.astype(y_ref.dtype)


def _diff_attention(qbd, k, vt, near_diag, near_prev, lq1, lk1, lq2, lk2, gs, lambda_init, n_groups, group_block):
    B, _, S, _ = k.shape
    T = ATT_TILE
    nt = S // T
    kern = functools.partial(_diff_kernel, n_groups=n_groups, lambda_init=lambda_init)
    small = lambda a: pl.BlockSpec(a.shape, lambda b, i: (0,) * a.ndim)
    return pl.pallas_call(
        kern,
        grid=(B, nt),
        in_specs=[small(lq1), small(lk1), small(lq2), small(lk2), small(gs), small(near_diag), small(near_prev)]
        + _attention_specs(n_groups, group_block, S),
        out_specs=pl.BlockSpec((1, T, n_groups * LANES), lambda b, i: (b, i, 0)),
        out_shape=jax.ShapeDtypeStruct((B, S, n_groups * LANES), BF16),
        scratch_shapes=[
            pltpu.VMEM((2, n_groups, T, 2 * T), F32),
            pltpu.VMEM((n_groups, 1, 2 * T), F32),
            pltpu.VMEM((n_groups, VT_ROWS, 2 * T), F32),
        ],
        compiler_params=pltpu.CompilerParams(
            dimension_semantics=("arbitrary", "arbitrary"), vmem_limit_bytes=VMEM_LIMIT),
        name="diff_attn",
    )(lq1, lk1, lq2, lk2, gs, near_diag, near_prev, qbd, k, vt)


def _sb_kernel(nu_ref, qbd_ref, k_ref, vt_ref, y_ref, z_ref, c_ref, ot_ref, *, n_groups):
    T = ATT_TILE
    qi = pl.program_id(1)
    groups = range(n_groups)
    kk = lax.broadcasted_iota(jnp.int32, (T, 2 * T), 0)
    qq = lax.broadcasted_iota(jnp.int32, (T, 2 * T), 1)
    causal = kk < jnp.where(qq >= T, qq - T, qq)
    sign = jnp.uint32(0x80000000)

    c_ref[...] = jnp.zeros_like(c_ref)
    ot_ref[...] = jnp.zeros_like(ot_ref)

    def score_group(g, j, slot):
        z_ref[slot, g] = _dot(_key_tile(k_ref, g, j), qbd_ref[0, 0, g])

    def tile(j, slot, masked, prefetch=True):
        sps = []
        for g in groups:
            z = z_ref[slot, g]
            neg_abs = lax.bitcast_convert_type(lax.bitcast_convert_type(z, jnp.uint32) | sign, F32)
            sp = jnp.maximum(z, 0.0) + jnp.log2(1.0 + jnp.exp2(neg_abs))
            if masked:
                sp = jnp.where(causal, sp, 0.0)
            sps.append(sp.astype(BF16))
        incls = []
        for g in groups:
            if prefetch:
                score_group(g, jnp.maximum(j - 1, 0), 1 - slot)
            incls.append(_dot(nu_ref[...], sps[g]))
        ws = []
        for g in groups:
            w = jnp.exp2((z_ref[slot, g] + c_ref[g]) + incls[g])
            if masked:
                w = jnp.where(causal, w, 0.0)
            ws.append(w.astype(BF16))
            c_ref[g] += incls[g][0:1]
        for g in groups:
            ot_ref[g] += _dot(vt_ref[0, j, g, :LANES], ws[g])

    for g in groups:
        score_group(g, qi, 0)
    tile(qi, 0, True)

    @pl.when(qi > 0)
    def _():
        tile(qi - 1, 1, False)

    def alive():
        return jnp.max(c_ref[...]) >= DEAD_LOG2

    rest = jnp.maximum(qi - 1, 0)
    n_pairs = lax.shift_right_logical(rest, 1)

    def pair(carry):
        t, _ = carry
        j = qi - 2 - 2 * t
        tile(j, 0, False)
        tile(j - 1, 1, False)
        return t + 1, alive()
    _, go = lax.while_loop(lambda carry: jnp.logical_and(carry[0] < n_pairs, carry[1]), pair,
                           (jnp.int32(0), alive()))

    @pl.when(jnp.logical_and(jnp.bitwise_and(rest, 1) == 1, go))
    def _():
        tile(0, 0, False, prefetch=False)

    for g in groups:
        ot = ot_ref[g]
        o = jnp.concatenate([ot[:HEAD_DIM, :T], ot[HEAD_DIM:, T:]], axis=0)
        y_ref[0, :, g * LANES:(g + 1) * LANES] = o.T.astype(y_ref.dtype)


def _sb_attention(qbd, k, vt, nu, n_groups, group_block):
    B, _, S, _ = k.shape
    T = ATT_TILE
    kern = functools.partial(_sb_kernel, n_groups=n_groups)
    return pl.pallas_call(
        kern,
        grid=(B, S // T),
        in_specs=[pl.BlockSpec(nu.shape, lambda b, i: (0, 0))] + _attention_specs(n_groups, group_block, S),
        out_specs=pl.BlockSpec((1, T, n_groups * LANES), lambda b, i: (b, i, 0)),
        out_shape=jax.ShapeDtypeStruct((B, S, n_groups * LANES), BF16),
        scratch_shapes=[
            pltpu.VMEM((2, n_groups, T, 2 * T), F32),
            pltpu.VMEM((n_groups, 1, 2 * T), F32),
            pltpu.VMEM((n_groups, LANES, 2 * T), F32),
        ],
        compiler_params=pltpu.CompilerParams(
            dimension_semantics=("arbitrary", "arbitrary"), vmem_limit_bytes=VMEM_LIMIT),
        name="sb_attn",
    )(nu, qbd, k, vt)


def _mix_kernel(x_ref, yd_ref, ys_ref, g_ref, wg_ref, wbd_ref, wbs_ref, wo_ref, o_ref):
    x = x_ref[...]
    D = x.shape[-1]
    h = _rms(x, g_ref[...]).astype(BF16)
    gates = jax.nn.sigmoid(_dot(h, wg_ref[...]))
    merged = gates[:, :D] * _dot(yd_ref[...], wbd_ref[...]) + gates[:, D:] * _dot(ys_ref[...], wbs_ref[...])
    o_ref[...] = x + _dot(merged.astype(BF16), wo_ref[...])


def _mix(x2d, yd, ys, g, wg, wbd, wbs, wo, tm):
    N, D = x2d.shape
    const = lambda a: pl.BlockSpec(a.shape, lambda i: (0, 0))
    row = lambda w: pl.BlockSpec((tm, w), lambda i: (i, 0))
    return pl.pallas_call(
        _mix_kernel,
        grid=(N // tm,),
        in_specs=[row(D), row(yd.shape[1]), row(ys.shape[1]), const(g), const(wg), const(wbd), const(wbs), const(wo)],
        out_specs=row(D),
        out_shape=jax.ShapeDtypeStruct((N, D), F32),
        compiler_params=pltpu.CompilerParams(dimension_semantics=("arbitrary",), vmem_limit_bytes=VMEM_LIMIT),
        name="mix",
    )(x2d, yd, ys, g, wg, wbd, wbs, wo)


def _ffn_kernel(x_ref, g_ref, wup_ref, cw_ref, cb_ref, wdn_ref, gf_ref, o_ref, abuf_ref, *, final_norm):
    tm = x_ref.shape[1]
    F = cw_ref.shape[1]
    H = SUBLANES

    @pl.when(pl.program_id(1) == 0)
    def _():
        abuf_ref[0:H] = jnp.zeros((H, F), F32)

    x = x_ref[0]
    h = _rms(x, g_ref[...]).astype(BF16)
    up = _dot(h, wup_ref[...])
    a = up[:, :F]
    abuf_ref[H:H + tm] = a
    conv = (cw_ref[0:1] * abuf_ref[H - 2:H - 2 + tm] + cw_ref[1:2] * abuf_ref[H - 1:H - 1 + tm]
            + cw_ref[2:3] * a + cb_ref[...])
    abuf_ref[0:H] = abuf_ref[tm:tm + H]
    gelu = 0.5 * conv * (1.0 + lax.erf(conv * (2.0 ** -0.5)))
    act = gelu * up[:, F:]
    x = x + _dot(act.astype(BF16), wdn_ref[...])
    o_ref[0] = _rms(x, gf_ref[...]) if final_norm else x


def _ffn(x, g, wup, cw, cb, wdn, gf, tm, final_norm):
    B, S, D = x.shape
    F = cw.shape[1]
    const = lambda a: pl.BlockSpec(a.shape, lambda b, i: (0, 0), pipeline_mode=pl.Buffered(1))
    return pl.pallas_call(
        functools.partial(_ffn_kernel, final_norm=final_norm),
        grid=(B, S // tm),
        in_specs=[pl.BlockSpec((1, tm, D), lambda b, i: (b, i, 0)),
                  const(g), const(wup), const(cw), const(cb), const(wdn), const(gf)],
        out_specs=pl.BlockSpec((1, tm, D), lambda b, i: (b, i, 0)),
        out_shape=jax.ShapeDtypeStruct((B, S, D), F32),
        scratch_shapes=[pltpu.VMEM((tm + SUBLANES, F), F32)],
        compiler_params=pltpu.CompilerParams(
            dimension_semantics=("arbitrary", "arbitrary"), vmem_limit_bytes=VMEM_LIMIT),
        name="ffn",
    )(x, g, wup, cw, cb, wdn, gf)


def _t5_bucket(rel):
    half = N_BUCKETS // 2
    ret = jnp.where(rel > 0, half, 0)
    n = jnp.abs(rel)
    max_exact = half // 2
    nf = jnp.maximum(n, 1).astype(jnp.float32)
    large = max_exact + (jnp.log(nf / max_exact) / math.log(MAX_DISTANCE / max_exact)
                         * (half - max_exact)).astype(jnp.int32)
    large = jnp.minimum(large, half - 1)
    return ret + jnp.where(n < max_exact, n, large)


def _near_bias_tables(rel_bias):
    T = ATT_TILE
    L = 2 * T
    H = rel_bias.shape[1]
    far = rel_bias[_t5_bucket(jnp.asarray(-MAX_DISTANCE))].astype(F32)
    m = jnp.arange(L)
    rel = jnp.where(m < T, -m, L - m)

    def toeplitz(rels):
        v = ((rel_bias[_t5_bucket(rels)].astype(F32) - far) * LOG2E).T
        rows = jnp.tile(v, (1, T))[:, :T * (L - 1)].reshape(H, T, L - 1)
        return rows[:, :, :T]

    kpos = jnp.arange(T)[:, None]
    qpos = jnp.arange(T)[None, :]
    visible = (kpos // CHUNK) <= (qpos // CHUNK)
    return jnp.where(visible[None], toeplitz(rel), -jnp.inf), toeplitz(rel - T)


def kernel(x, norm_mix_g, w_in, diff_lambda_q1, diff_lambda_k1, diff_lambda_q2, diff_lambda_k2, diff_subln_g,
           rel_bias, w_branch_diff, w_branch_sb, w_out, norm_ffn_g, w_ffn_up, ffn_conv_w, ffn_conv_b, w_ffn_down,
           norm_final_g):
    B, S, D = x.shape
    depth = w_in.shape[0]
    T = ATT_TILE
    n_heads_diff = rel_bias.shape[1]
    qk_w = n_heads_diff * 2 * HEAD_DIM
    ngd = qk_w // LANES
    assert S % T == 0 and T % CHUNK == 0 and T >= MAX_DISTANCE

    near_diag, near_prev = _near_bias_tables(rel_bias)
    neg_upper = -(jnp.arange(T)[None, :] >= jnp.arange(T)[:, None]).astype(BF16)

    for layer in range(depth):
        lambda_init = 0.8 - 0.6 * math.exp(-0.3 * layer)
        w = w_in[layer]
        cols = lambda i: w[:, i * qk_w:(i + 1) * qk_w]
        wk = jnp.concatenate([cols(1), cols(4)], axis=1).astype(BF16)
        wqvt = jnp.concatenate([cols(0), cols(3), cols(2), cols(5)], axis=1).T.astype(BF16)
        wg = w[:, 6 * qk_w:].astype(BF16)

        k, qbd, vt = _proj(x, norm_mix_g[layer][None], wk, wqvt, 2 * ngd, tiles_per_step=2)
        row = lambda a: a[layer][None].astype(F32)
        y_diff = _diff_attention(qbd, k, vt, near_diag, near_prev, row(diff_lambda_q1), row(diff_lambda_k1),
                                 row(diff_lambda_q2), row(diff_lambda_k2), row(diff_subln_g), lambda_init, ngd, 0)
        y_sb = _sb_attention(qbd, k, vt, neg_upper, ngd, 1)

        x = _mix(x.reshape(B * S, D), y_diff.reshape(B * S, -1), y_sb.reshape(B * S, -1), norm_mix_g[layer][None],
                 wg, w_branch_diff[layer].astype(BF16), w_branch_sb[layer].astype(BF16), w_out[layer].astype(BF16),
                 tm=512).reshape(B, S, D)
        x = _ffn(x, norm_ffn_g[layer][None], w_ffn_up[layer].astype(BF16), ffn_conv_w[layer], ffn_conv_b[layer][None],
                 w_ffn_down[layer].astype(BF16), norm_final_g[None], tm=512, final_norm=layer == depth - 1)
    return x
```

```python
import functools
import math

import jax
import jax.numpy as jnp
from jax import lax
from jax.experimental import pallas as pl
from jax.experimental.pallas import tpu as pltpu

F32 = jnp.float32
BF16 = jnp.bfloat16

HEAD_DIM = 64
CHUNK = 64
N_BUCKETS = 32
MAX_DISTANCE = 128
NORM_EPS = 1e-6
LOG2E = math.log2(math.e)
DEAD_LOG2 = -150.0

LANES = 128
SUBLANES = 8
ATT_TILE = 256
VMEM_LIMIT = 56 * 1024 * 1024


def _dot(a, b):
    return jnp.dot(a, b, preferred_element_type=F32)


def _rms(x, g):
    return x * lax.rsqrt(jnp.mean(x * x, axis=-1, keepdims=True) + NORM_EPS) * g


def _proj_kernel(x_ref, g_ref, wk_ref, wqvt_ref, k_ref, qbd_ref, vt_ref, *, n_groups, scale):
    T = ATT_TILE
    tiles = x_ref.shape[1] // T
    h = _rms(x_ref[0], g_ref[...]).astype(BF16)
    kk = _dot(h, wk_ref[...]).astype(BF16)
    for g in range(n_groups):
        k_ref[0, g] = kk[:, g * LANES:(g + 1) * LANES]
    qv = lax.dot_general(wqvt_ref[...], h, (((1,), (1,)), ((), ())), preferred_element_type=F32)
    nq = n_groups * LANES
    first = lax.broadcasted_iota(jnp.int32, (LANES, T), 0) < HEAD_DIM
    zero = jnp.zeros((LANES, T), BF16)
    for t in range(tiles):
        for g in range(n_groups):
            qg = (qv[g * LANES:(g + 1) * LANES, t * T:(t + 1) * T] * scale).astype(BF16)
            qbd_ref[0, t, g] = jnp.concatenate([jnp.where(first, qg, zero), jnp.where(first, zero, qg)], axis=1)
            vt_ref[0, t, g] = qv[nq + g * LANES:nq + (g + 1) * LANES, t * T:(t + 1) * T].astype(BF16)


def _proj(x, g, wk, wqvt, n_groups, tiles_per_step):
    B, S, D = x.shape
    T = ATT_TILE
    nt = S // T
    tps = tiles_per_step
    kern = functools.partial(_proj_kernel, n_groups=n_groups, scale=HEAD_DIM ** -0.5 * LOG2E)
    return pl.pallas_call(
        kern,
        grid=(B, nt // tps),
        in_specs=[
            pl.BlockSpec((1, tps * T, D), lambda b, i: (b, i, 0)),
            pl.BlockSpec((1, D), lambda b, i: (0, 0)),
            pl.BlockSpec(wk.shape, lambda b, i: (0, 0)),
            pl.BlockSpec(wqvt.shape, lambda b, i: (0, 0)),
        ],
        out_specs=[
            pl.BlockSpec((1, n_groups, tps * T, LANES), lambda b, i: (b, 0, i, 0)),
            pl.BlockSpec((1, tps, n_groups, LANES, 2 * T), lambda b, i: (b, i, 0, 0, 0)),
            pl.BlockSpec((1, tps, n_groups, LANES, T), lambda b, i: (b, i, 0, 0, 0)),
        ],
        out_shape=[
            jax.ShapeDtypeStruct((B, n_groups, S, LANES), BF16),
            jax.ShapeDtypeStruct((B, nt, n_groups, LANES, 2 * T), BF16),
            jax.ShapeDtypeStruct((B, nt, n_groups, LANES, T), BF16),
        ],
        compiler_params=pltpu.CompilerParams(
            dimension_semantics=("arbitrary", "arbitrary"), vmem_limit_bytes=VMEM_LIMIT),
        name="proj",
    )(x, g, wk, wqvt)


def _key_tile(k_ref, g, j):
    T = ATT_TILE
    start = j * T if isinstance(j, int) else pl.multiple_of(j * T, T)
    return k_ref[0, g, pl.ds(start, T), :]


def _attention_specs(n_groups, group_block, S):
    T = ATT_TILE
    gb = group_block
    return [
        pl.BlockSpec((1, S // T, n_groups, LANES, 2 * T), lambda b: (b, 0, gb, 0, 0)),
        pl.BlockSpec((1, n_groups, S, LANES), lambda b: (b, gb, 0, 0)),
        pl.BlockSpec((1, S // T, n_groups, LANES, T), lambda b: (b, 0, gb, 0, 0)),
    ]


def _diff_kernel(lq1_ref, lk1_ref, lq2_ref, lk2_ref, gs_ref, nbd_ref, nbp_ref, qbd_ref, k_ref, vt_ref,
                 y_ref, z_ref, mx_ref, l_ref, ot_ref, *, n_groups, lambda_init):
    T = ATT_TILE
    nt = qbd_ref.shape[1]
    groups = range(n_groups)
    lam = (jnp.exp(jnp.sum(lq1_ref[...] * lk1_ref[...], keepdims=True))
           - jnp.exp(jnp.sum(lq2_ref[...] * lk2_ref[...], keepdims=True)) + lambda_init)

    def score_group(g, qi, j, slot, nb_ref):
        st = _dot(_key_tile(k_ref, g, j), qbd_ref[0, qi, g])
        if nb_ref is not None:
            nb = nb_ref[g]
            st = jnp.concatenate([st[:, :T] + nb, st[:, T:] + nb], axis=1)
        z_ref[slot, g] = st

    def tile(qi, j, slot, first=False, next_bias=None, prefetch=True):
        ps, alphas = [], []
        for g in groups:
            s = z_ref[slot, g]
            m = jnp.max(s, axis=0, keepdims=True)
            if not first:
                m_old = mx_ref[g]
                m = jnp.maximum(m_old, m)
                alphas.append(jnp.exp2(m_old - m))
            p = jnp.exp2(s - m)
            lsum = jnp.sum(p, axis=0, keepdims=True)
            l_ref[g] = lsum if first else alphas[g] * l_ref[g] + lsum
            mx_ref[g] = m
            ps.append(p.astype(BF16))
        for g in groups:
            if prefetch:
                score_group(g, qi, jnp.maximum(j - 1, 0), 1 - slot, next_bias)
            pv = _dot(vt_ref[0, j, g], ps[g])
            ot_ref[g] = pv if first else alphas[g] * ot_ref[g] + pv

    def query_tile(qi, carry):
        tile(qi, qi, 0, first=True, next_bias=nbp_ref)

        @pl.when(qi > 0)
        def _():
            tile(qi, qi - 1, 1)

        rest = jnp.maximum(qi - 1, 0)

        def pair(t, c):
            j = qi - 2 - 2 * t
            tile(qi, j, 0)
            tile(qi, j - 1, 1)
            return c
        lax.fori_loop(0, lax.shift_right_logical(rest, 1), pair, 0)

        @pl.when(jnp.bitwise_and(rest, 1) == 1)
        def _():
            tile(qi, 0, 0, prefetch=False)

        nxt = jnp.minimum(qi + 1, nt - 1)
        for g in groups:
            score_group(g, nxt, nxt, 0, nbd_ref)
        for g in groups:
            ot = ot_ref[g] * (1.0 / l_ref[g])
            o = ot[:, :T] - lam * ot[:, T:]
            o = o * lax.rsqrt(jnp.mean(o * o, axis=0, keepdims=True) + NORM_EPS)
            y = o.T * gs_ref[...] * (1.0 - lambda_init)
            y_ref[0, pl.ds(pl.multiple_of(qi * T, T), T), g * LANES:(g + 1) * LANES] = y.astype(y_ref.dtype)
        return carry

    for g in groups:
        score_group(g, 0, 0, 0, nbd_ref)
    lax.fori_loop(0, nt, query_tile, 0)


def _diff_attention(qbd, k, vt, near_diag, near_prev, lq1, lk1, lq2, lk2, gs, lambda_init, n_groups, group_block):
    B, _, S, _ = k.shape
    T = ATT_TILE
    kern = functools.partial(_diff_kernel, n_groups=n_groups, lambda_init=lambda_init)
    small = lambda a: pl.BlockSpec(a.shape, lambda b: (0,) * a.ndim)
    return pl.pallas_call(
        kern,
        grid=(B,),
        in_specs=[small(lq1), small(lk1), small(lq2), small(lk2), small(gs), small(near_diag), small(near_prev)]
        + _attention_specs(n_groups, group_block, S),
        out_specs=pl.BlockSpec((1, S, n_groups * LANES), lambda b: (b, 0, 0)),
        out_shape=jax.ShapeDtypeStruct((B, S, n_groups * LANES), BF16),
        scratch_shapes=[
            pltpu.VMEM((2, n_groups, T, 2 * T), F32),
            pltpu.VMEM((n_groups, 1, 2 * T), F32),
            pltpu.VMEM((n_groups, 1, 2 * T), F32),
            pltpu.VMEM((n_groups, LANES, 2 * T), F32),
        ],
        compiler_params=pltpu.CompilerParams(dimension_semantics=("arbitrary",), vmem_limit_bytes=VMEM_LIMIT),
        name="diff_attn",
    )(lq1, lk1, lq2, lk2, gs, near_diag, near_prev, qbd, k, vt)


def _sb_kernel(nu_ref, qbd_ref, k_ref, vt_ref, y_ref, z_ref, c_ref, ot_ref, *, n_groups):
    T = ATT_TILE
    nt = qbd_ref.shape[1]
    groups = range(n_groups)
    kk = lax.broadcasted_iota(jnp.int32, (T, 2 * T), 0)
    qq = lax.broadcasted_iota(jnp.int32, (T, 2 * T), 1)
    causal = kk < jnp.where(qq >= T, qq - T, qq)
    sign = jnp.uint32(0x80000000)

    def score_group(g, qi, j, slot):
        z_ref[slot, g] = _dot(_key_tile(k_ref, g, j), qbd_ref[0, qi, g])

    def tile(qi, j, slot, masked, prefetch=True):
        sps = []
        for g in groups:
            z = z_ref[slot, g]
            neg_abs = lax.bitcast_convert_type(lax.bitcast_convert_type(z, jnp.uint32) | sign, F32)
            sp = jnp.maximum(z, 0.0) + jnp.log2(1.0 + jnp.exp2(neg_abs))
            if masked:
                sp = jnp.where(causal, sp, 0.0)
            sps.append(sp.astype(BF16))
        incls = []
        for g in groups:
            if prefetch:
                score_group(g, qi, jnp.maximum(j - 1, 0), 1 - slot)
            incls.append(_dot(nu_ref[...], sps[g]))
        ws = []
        for g in groups:
            w = jnp.exp2((z_ref[slot, g] + c_ref[g]) + incls[g])
            if masked:
                w = jnp.where(causal, w, 0.0)
            ws.append(w.astype(BF16))
            c_ref[g] += incls[g][0:1]
        for g in groups:
            ot_ref[g] += _dot(vt_ref[0, j, g], ws[g])

    def alive():
        return jnp.max(c_ref[...]) >= DEAD_LOG2

    def query_tile(qi, carry):
        c_ref[...] = jnp.zeros_like(c_ref)
        ot_ref[...] = jnp.zeros_like(ot_ref)
        tile(qi, qi, 0, True)

        @pl.when(qi > 0)
        def _():
            tile(qi, qi - 1, 1, False)

        rest = jnp.maximum(qi - 1, 0)
        n_pairs = lax.shift_right_logical(rest, 1)

        def pair(c):
            t, _ = c
            j = qi - 2 - 2 * t
            tile(qi, j, 0, False)
            tile(qi, j - 1, 1, False)
            return t + 1, alive()
        _, go = lax.while_loop(lambda c: jnp.logical_and(c[0] < n_pairs, c[1]), pair, (jnp.int32(0), alive()))

        @pl.when(jnp.logical_and(jnp.bitwise_and(rest, 1) == 1, go))
        def _():
            tile(qi, 0, 0, False, prefetch=False)

        nxt = jnp.minimum(qi + 1, nt - 1)
        for g in groups:
            score_group(g, nxt, nxt, 0)
        for g in groups:
            ot = ot_ref[g]
            o = jnp.concatenate([ot[:HEAD_DIM, :T], ot[HEAD_DIM:, T:]], axis=0)
            y_ref[0, pl.ds(pl.multiple_of(qi * T, T), T), g * LANES:(g + 1) * LANES] = o.T.astype(y_ref.dtype)
        return carry

    for g in groups:
        score_group(g, 0, 0, 0)
    lax.fori_loop(0, nt, query_tile, 0)


def _sb_attention(qbd, k, vt, nu, n_groups, group_block):
    B, _, S, _ = k.shape
    T = ATT_TILE
    kern = functools.partial(_sb_kernel, n_groups=n_groups)
    return pl.pallas_call(
        kern,
        grid=(B,),
        in_specs=[pl.BlockSpec(nu.shape, lambda b: (0, 0))] + _attention_specs(n_groups, group_block, S),
        out_specs=pl.BlockSpec((1, S, n_groups * LANES), lambda b: (b, 0, 0)),
        out_shape=jax.ShapeDtypeStruct((B, S, n_groups * LANES), BF16),
        scratch_shapes=[
            pltpu.VMEM((2, n_groups, T, 2 * T), F32),
            pltpu.VMEM((n_groups, 1, 2 * T), F32),
            pltpu.VMEM((n_groups, LANES, 2 * T), F32),
        ],
        compiler_params=pltpu.CompilerParams(dimension_semantics=("arbitrary",), vmem_limit_bytes=VMEM_LIMIT),
        name="sb_attn",
    )(nu, qbd, k, vt)


def _mix_kernel(x_ref, yd_ref, ys_ref, g_ref, wg_ref, wbd_ref, wbs_ref, wo_ref, o_ref):
    x = x_ref[...]
    D = x.shape[-1]
    h = _rms(x, g_ref[...]).astype(BF16)
    gates = jax.nn.sigmoid(_dot(h, wg_ref[...]))
    merged = gates[:, :D] * _dot(yd_ref[...], wbd_ref[...]) + gates[:, D:] * _dot(ys_ref[...], wbs_ref[...])
    o_ref[...] = x + _dot(merged.astype(BF16), wo_ref[...])


def _mix(x2d, yd, ys, g, wg, wbd, wbs, wo, tm):
    N, D = x2d.shape
    const = lambda a: pl.BlockSpec(a.shape, lambda i: (0, 0))
    row = lambda w: pl.BlockSpec((tm, w), lambda i: (i, 0))
    return pl.pallas_call(
        _mix_kernel,
        grid=(N // tm,),
        in_specs=[row(D), row(yd.shape[1]), row(ys.shape[1]), const(g), const(wg), const(wbd), const(wbs), const(wo)],
        out_specs=row(D),
        out_shape=jax.ShapeDtypeStruct((N, D), F32),
        compiler_params=pltpu.CompilerParams(dimension_semantics=("arbitrary",), vmem_limit_bytes=VMEM_LIMIT),
        name="mix",
    )(x2d, yd, ys, g, wg, wbd, wbs, wo)


def _ffn_kernel(x_ref, g_ref, wup_ref, cw_ref, cb_ref, wdn_ref, gf_ref, o_ref, abuf_ref, *, final_norm):
    tm = x_ref.shape[1]
    F = cw_ref.shape[1]
    H = SUBLANES

    @pl.when(pl.program_id(1) == 0)
    def _():
        abuf_ref[0:H] = jnp.zeros((H, F), F32)

    x = x_ref[0]
    h = _rms(x, g_ref[...]).astype(BF16)
    up = _dot(h, wup_ref[...])
    a = up[:, :F]
    abuf_ref[H:H + tm] = a
    conv = (cw_ref[0:1] * abuf_ref[H - 2:H - 2 + tm] + cw_ref[1:2] * abuf_ref[H - 1:H - 1 + tm]
            + cw_ref[2:3] * a + cb_ref[...])
    abuf_ref[0:H] = abuf_ref[tm:tm + H]
    gelu = 0.5 * conv * (1.0 + lax.erf(conv * (2.0 ** -0.5)))
    act = gelu * up[:, F:]
    x = x + _dot(act.astype(BF16), wdn_ref[...])
    o_ref[0] = _rms(x, gf_ref[...]) if final_norm else x


def _ffn(x, g, wup, cw, cb, wdn, gf, tm, final_norm):
    B, S, D = x.shape
    F = cw.shape[1]
    const = lambda a: pl.BlockSpec(a.shape, lambda b, i: (0, 0), pipeline_mode=pl.Buffered(1))
    return pl.pallas_call(
        functools.partial(_ffn_kernel, final_norm=final_norm),
        grid=(B, S // tm),
        in_specs=[pl.BlockSpec((1, tm, D), lambda b, i: (b, i, 0)),
                  const(g), const(wup), const(cw), const(cb), const(wdn), const(gf)],
        out_specs=pl.BlockSpec((1, tm, D), lambda b, i: (b, i, 0)),
        out_shape=jax.ShapeDtypeStruct((B, S, D), F32),
        scratch_shapes=[pltpu.VMEM((tm + SUBLANES, F), F32)],
        compiler_params=pltpu.CompilerParams(
            dimension_semantics=("arbitrary", "arbitrary"), vmem_limit_bytes=VMEM_LIMIT),
        name="ffn",
    )(x, g, wup, cw, cb, wdn, gf)


def _t5_bucket(rel):
    half = N_BUCKETS // 2
    ret = jnp.where(rel > 0, half, 0)
    n = jnp.abs(rel)
    max_exact = half // 2
    nf = jnp.maximum(n, 1).astype(jnp.float32)
    large = max_exact + (jnp.log(nf / max_exact) / math.log(MAX_DISTANCE / max_exact)
                         * (half - max_exact)).astype(jnp.int32)
    large = jnp.minimum(large, half - 1)
    return ret + jnp.where(n < max_exact, n, large)


def _near_bias_tables(rel_bias):
    T = ATT_TILE
    L = 2 * T
    H = rel_bias.shape[1]
    far = rel_bias[_t5_bucket(jnp.asarray(-MAX_DISTANCE))].astype(F32)
    m = jnp.arange(L)
    rel = jnp.where(m < T, -m, L - m)

    def toeplitz(rels):
        v = ((rel_bias[_t5_bucket(rels)].astype(F32) - far) * LOG2E).T
        rows = jnp.tile(v, (1, T))[:, :T * (L - 1)].reshape(H, T, L - 1)
        return rows[:, :, :T]

    kpos = jnp.arange(T)[:, None]
    qpos = jnp.arange(T)[None, :]
    visible = (kpos // CHUNK) <= (qpos // CHUNK)
    return jnp.where(visible[None], toeplitz(rel), -jnp.inf), toeplitz(rel - T)


def kernel(x, norm_mix_g, w_in, diff_lambda_q1, diff_lambda_k1, diff_lambda_q2, diff_lambda_k2, diff_subln_g,
           rel_bias, w_branch_diff, w_branch_sb, w_out, norm_ffn_g, w_ffn_up, ffn_conv_w, ffn_conv_b, w_ffn_down,
           norm_final_g):
    B, S, D = x.shape
    depth = w_in.shape[0]
    T = ATT_TILE
    n_heads_diff = rel_bias.shape[1]
    qk_w = n_heads_diff * 2 * HEAD_DIM
    ngd = qk_w // LANES
    assert S % T == 0 and T % CHUNK == 0 and T >= MAX_DISTANCE

    near_diag, near_prev = _near_bias_tables(rel_bias)
    neg_upper = -(jnp.arange(T)[None, :] >= jnp.arange(T)[:, None]).astype(BF16)

    for layer in range(depth):
        lambda_init = 0.8 - 0.6 * math.exp(-0.3 * layer)
        w = w_in[layer]
        cols = lambda i: w[:, i * qk_w:(i + 1) * qk_w]
        wk = jnp.concatenate([cols(1), cols(4)], axis=1).astype(BF16)
        wqvt = jnp.concatenate([cols(0), cols(3), cols(2), cols(5)], axis=1).T.astype(BF16)
        wg = w[:, 6 * qk_w:].astype(BF16)

        k, qbd, vt = _proj(x, norm_mix_g[layer][None], wk, wqvt, 2 * ngd, tiles_per_step=2)
        row = lambda a: a[layer][None].astype(F32)
        y_diff = _diff_attention(qbd, k, vt, near_diag, near_prev, row(diff_lambda_q1), row(diff_lambda_k1),
                                 row(diff_lambda_q2), row(diff_lambda_k2), row(diff_subln_g), lambda_init, ngd, 0)
        y_sb = _sb_attention(qbd, k, vt, neg_upper, ngd, 1)

        x = _mix(x.reshape(B * S, D), y_diff.reshape(B * S, -1), y_sb.reshape(B * S, -1), norm_mix_g[layer][None],
                 wg, w_branch_diff[layer].astype(BF16), w_branch_sb[layer].astype(BF16), w_out[layer].astype(BF16),
                 tm=512).reshape(B, S, D)
        x = _ffn(x, norm_ffn_g[layer][None], w_ffn_up[layer].astype(BF16), ffn_conv_w[layer], ffn_conv_b[layer][None],
                 w_ffn_down[layer].astype(BF16), norm_final_g[None], tm=512, final_norm=layer == depth - 1)
    return x
```

```python
import functools
import math

import jax
import jax.numpy as jnp
from jax import lax
from jax.experimental import pallas as pl
from jax.experimental.pallas import tpu as pltpu

F32 = jnp.float32
BF16 = jnp.bfloat16

HEAD_DIM = 64
CHUNK = 64
N_BUCKETS = 32
MAX_DISTANCE = 128
NORM_EPS = 1e-6
LOG2E = math.log2(math.e)
DEAD_LOG2 = -150.0

LANES = 128
SUBLANES = 8
BF16_ROWS = 16
ATT_TILE = 256
VMEM_LIMIT = 56 * 1024 * 1024


def _dot(a, b):
    return jnp.dot(a, b, preferred_element_type=F32)


def _rms(x, g):
    return x * lax.rsqrt(jnp.mean(x * x, axis=-1, keepdims=True) + NORM_EPS) * g


def _proj_kernel(x_ref, g_ref, wk_ref, wqvt_ref, *refs, n_groups, scale, n_cast):
    cast_in, (k_ref, qbd_ref, vt_ref), cast_out = refs[:n_cast], refs[n_cast:n_cast + 3], refs[n_cast + 3:]
    T = ATT_TILE
    tiles = x_ref.shape[1] // T
    h = _rms(x_ref[0], g_ref[...]).astype(BF16)
    kk = _dot(h, wk_ref[...]).astype(BF16)
    for g in range(n_groups):
        k_ref[0, g] = kk[:, g * LANES:(g + 1) * LANES]
    qv = lax.dot_general(wqvt_ref[...], h, (((1,), (1,)), ((), ())), preferred_element_type=F32)
    nq = n_groups * LANES
    first = lax.broadcasted_iota(jnp.int32, (LANES, T), 0) < HEAD_DIM
    zero = jnp.zeros((LANES, T), BF16)
    for t in range(tiles):
        for g in range(n_groups):
            qg = (qv[g * LANES:(g + 1) * LANES, t * T:(t + 1) * T] * scale).astype(BF16)
            qbd_ref[0, t, g] = jnp.concatenate([jnp.where(first, qg, zero), jnp.where(first, zero, qg)], axis=1)
            vt_ref[0, t, g] = qv[nq + g * LANES:nq + (g + 1) * LANES, t * T:(t + 1) * T].astype(BF16)
    for src, dst in zip(cast_in, cast_out):
        dst[...] = src[...].astype(BF16)


def _proj(x, g, wk, wqvt, n_groups, tiles_per_step, cast_weights):
    B, S, D = x.shape
    T = ATT_TILE
    nt = S // T
    tps = tiles_per_step
    steps_per_batch = nt // tps
    n_steps = B * steps_per_batch
    cast_in_specs, cast_out_specs, cast_out_shapes = [], [], []
    for w, width, col in cast_weights:
        rows = w.shape[0] // n_steps
        assert rows * n_steps == w.shape[0] and rows % BF16_ROWS == 0 and w.shape[1] % width == 0
        cast_in_specs.append(pl.BlockSpec((rows, width), lambda b, i, col=col: (b * steps_per_batch + i, col)))
        cast_out_specs.append(pl.BlockSpec((rows, width), lambda b, i: (b * steps_per_batch + i, 0)))
        cast_out_shapes.append(jax.ShapeDtypeStruct((w.shape[0], width), BF16))
    kern = functools.partial(_proj_kernel, n_groups=n_groups, scale=HEAD_DIM ** -0.5 * LOG2E,
                             n_cast=len(cast_weights))
    outs = pl.pallas_call(
        kern,
        grid=(B, steps_per_batch),
        in_specs=[
            pl.BlockSpec((1, tps * T, D), lambda b, i: (b, i, 0)),
            pl.BlockSpec((1, D), lambda b, i: (0, 0)),
            pl.BlockSpec(wk.shape, lambda b, i: (0, 0)),
            pl.BlockSpec(wqvt.shape, lambda b, i: (0, 0)),
        ] + cast_in_specs,
        out_specs=[
            pl.BlockSpec((1, n_groups, tps * T, LANES), lambda b, i: (b, 0, i, 0)),
            pl.BlockSpec((1, tps, n_groups, LANES, 2 * T), lambda b, i: (b, i, 0, 0, 0)),
            pl.BlockSpec((1, tps, n_groups, LANES, T), lambda b, i: (b, i, 0, 0, 0)),
        ] + cast_out_specs,
        out_shape=[
            jax.ShapeDtypeStruct((B, n_groups, S, LANES), BF16),
            jax.ShapeDtypeStruct((B, nt, n_groups, LANES, 2 * T), BF16),
            jax.ShapeDtypeStruct((B, nt, n_groups, LANES, T), BF16),
        ] + cast_out_shapes,
        compiler_params=pltpu.CompilerParams(
            dimension_semantics=("arbitrary", "arbitrary"), vmem_limit_bytes=VMEM_LIMIT),
        name="proj",
    )(x, g, wk, wqvt, *[w for w, _, _ in cast_weights])
    return outs[:3], outs[3:]


def _key_tile(k_ref, g, j):
    T = ATT_TILE
    start = j * T if isinstance(j, int) else pl.multiple_of(j * T, T)
    return k_ref[0, g, pl.ds(start, T), :]


def _attention_specs(n_groups, group_block, S):
    T = ATT_TILE
    gb = group_block
    return [
        pl.BlockSpec((1, 1, n_groups, LANES, 2 * T), lambda b, i: (b, i, gb, 0, 0)),
        pl.BlockSpec((1, n_groups, S, LANES), lambda b, i: (b, gb, 0, 0)),
        pl.BlockSpec((1, S // T, n_groups, LANES, T), lambda b, i: (b, 0, gb, 0, 0)),
    ]


def _diff_kernel(lq1_ref, lk1_ref, lq2_ref, lk2_ref, gs_ref, nbd_ref, nbp_ref, qbd_ref, k_ref, vt_ref,
                 y_ref, z_ref, mx_ref, l_ref, ot_ref, *, n_groups, lambda_init):
    T = ATT_TILE
    qi = pl.program_id(1)
    groups = range(n_groups)

    def score_group(g, j, slot, nb_ref):
        st = _dot(_key_tile(k_ref, g, j), qbd_ref[0, 0, g])
        if nb_ref is not None:
            nb = nb_ref[g]
            st = jnp.concatenate([st[:, :T] + nb, st[:, T:] + nb], axis=1)
        z_ref[slot, g] = st

    def tile(j, slot, first=False, next_bias=None, prefetch=True):
        ps, alphas = [], []
        for g in groups:
            s = z_ref[slot, g]
            m = jnp.max(s, axis=0, keepdims=True)
            if not first:
                m_old = mx_ref[g]
                m = jnp.maximum(m_old, m)
                alphas.append(jnp.exp2(m_old - m))
            p = jnp.exp2(s - m)
            lsum = jnp.sum(p, axis=0, keepdims=True)
            l_ref[g] = lsum if first else alphas[g] * l_ref[g] + lsum
            mx_ref[g] = m
            ps.append(p.astype(BF16))
        for g in groups:
            if prefetch:
                score_group(g, jnp.maximum(j - 1, 0), 1 - slot, next_bias)
            pv = _dot(vt_ref[0, j, g], ps[g])
            ot_ref[g] = pv if first else alphas[g] * ot_ref[g] + pv

    for g in groups:
        score_group(g, qi, 0, nbd_ref)
    tile(qi, 0, first=True, next_bias=nbp_ref)

    @pl.when(qi > 0)
    def _():
        tile(qi - 1, 1)

    rest = jnp.maximum(qi - 1, 0)

    def pair(t, carry):
        j = qi - 2 - 2 * t
        tile(j, 0)
        tile(j - 1, 1)
        return carry
    lax.fori_loop(0, lax.shift_right_logical(rest, 1), pair, 0)

    @pl.when(jnp.bitwise_and(rest, 1) == 1)
    def _():
        tile(0, 0, prefetch=False)

    lam = (jnp.exp(jnp.sum(lq1_ref[...] * lk1_ref[...], keepdims=True))
           - jnp.exp(jnp.sum(lq2_ref[...] * lk2_ref[...], keepdims=True)) + lambda_init)
    for g in groups:
        ot = ot_ref[g] * (1.0 / l_ref[g])
        o = ot[:, :T] - lam * ot[:, T:]
        o = o * lax.rsqrt(jnp.mean(o * o, axis=0, keepdims=True) + NORM_EPS)
        y = o.T * gs_ref[...] * (1.0 - lambda_init)
        y_ref[0, :, g * LANES:(g + 1) * LANES] = y.astype(y_ref.dtype)


def _diff_attention(qbd, k, vt, near_diag, near_prev, lq1, lk1, lq2, lk2, gs, lambda_init, n_groups, group_block):
    B, _, S, _ = k.shape
    T = ATT_TILE
    nt = S // T
    kern = functools.partial(_diff_kernel, n_groups=n_groups, lambda_init=lambda_init)
    small = lambda a: pl.BlockSpec(a.shape, lambda b, i: (0,) * a.ndim)
    return pl.pallas_call(
        kern,
        grid=(B, nt),
        in_specs=[small(lq1), small(lk1), small(lq2), small(lk2), small(gs), small(near_diag), small(near_prev)]
        + _attention_specs(n_groups, group_block, S),
        out_specs=pl.BlockSpec((1, T, n_groups * LANES), lambda b, i: (b, i, 0)),
        out_shape=jax.ShapeDtypeStruct((B, S, n_groups * LANES), BF16),
        scratch_shapes=[
            pltpu.VMEM((2, n_groups, T, 2 * T), F32),
            pltpu.VMEM((n_groups, 1, 2 * T), F32),
            pltpu.VMEM((n_groups, 1, 2 * T), F32),
            pltpu.VMEM((n_groups, LANES, 2 * T), F32),
        ],
        compiler_params=pltpu.CompilerParams(
            dimension_semantics=("arbitrary", "arbitrary"), vmem_limit_bytes=VMEM_LIMIT),
        name="diff_attn",
    )(lq1, lk1, lq2, lk2, gs, near_diag, near_prev, qbd, k, vt)


def _sb_kernel(nu_ref, qbd_ref, k_ref, vt_ref, y_ref, z_ref, c_ref, ot_ref, *, n_groups):
    T = ATT_TILE
    qi = pl.program_id(1)
    groups = range(n_groups)
    kk = lax.broadcasted_iota(jnp.int32, (T, 2 * T), 0)
    qq = lax.broadcasted_iota(jnp.int32, (T, 2 * T), 1)
    causal = kk < jnp.where(qq >= T, qq - T, qq)
    sign = jnp.uint32(0x80000000)

    c_ref[...] = jnp.zeros_like(c_ref)
    ot_ref[...] = jnp.zeros_like(ot_ref)

    def score_tile(j, slot):
        for g in groups:
            z_ref[slot, g] = _dot(_key_tile(k_ref, g, j), qbd_ref[0, 0, g])

    def tile(j, slot, masked, prefetch):
        sps = []
        for g in groups:
            z = z_ref[slot, g]
            neg_abs = lax.bitcast_convert_type(lax.bitcast_convert_type(z, jnp.uint32) | sign, F32)
            sp = jnp.maximum(z, 0.0) + jnp.log2(1.0 + jnp.exp2(neg_abs))
            if masked:
                sp = jnp.where(causal, sp, 0.0)
            sps.append(sp.astype(BF16))
        incls = []
        for g in groups:
            if prefetch:
                z_ref[1 - slot, g] = _dot(_key_tile(k_ref, g, jnp.maximum(j - 1, 0)), qbd_ref[0, 0, g])
            incls.append(_dot(nu_ref[...], sps[g]))
        ws = []
        for g in groups:
            w = jnp.exp2((z_ref[slot, g] + c_ref[g]) + incls[g])
            if masked:
                w = jnp.where(causal, w, 0.0)
            ws.append(w.astype(BF16))
            c_ref[g] += incls[g][0:1]
        for g in groups:
            ot_ref[g] += _dot(vt_ref[0, j, g], ws[g])

    score_tile(qi, 0)
    tile(qi, 0, True, prefetch=True)

    @pl.when(qi > 0)
    def _():
        tile(qi - 1, 1, False, prefetch=False)

    def alive():
        return jnp.max(c_ref[...]) >= DEAD_LOG2

    rest = jnp.maximum(qi - 1, 0)
    n_pairs = lax.shift_right_logical(rest, 1)

    def pair(carry):
        t, _ = carry
        j = qi - 2 - 2 * t
        score_tile(j, 0)
        tile(j, 0, False, prefetch=True)
        tile(j - 1, 1, False, prefetch=False)
        return t + 1, alive()
    _, go = lax.while_loop(lambda carry: jnp.logical_and(carry[0] < n_pairs, carry[1]), pair,
                           (jnp.int32(0), alive()))

    @pl.when(jnp.logical_and(jnp.bitwise_and(rest, 1) == 1, go))
    def _():
        score_tile(0, 0)
        tile(0, 0, False, prefetch=False)

    for g in groups:
        ot = ot_ref[g]
        o = jnp.concatenate([ot[:HEAD_DIM, :T], ot[HEAD_DIM:, T:]], axis=0)
        y_ref[0, :, g * LANES:(g + 1) * LANES] = o.T.astype(y_ref.dtype)


def _sb_attention(qbd, k, vt, nu, n_groups, group_block):
    B, _, S, _ = k.shape
    T = ATT_TILE
    kern = functools.partial(_sb_kernel, n_groups=n_groups)
    return pl.pallas_call(
        kern,
        grid=(B, S // T),
        in_specs=[pl.BlockSpec(nu.shape, lambda b, i: (0, 0))] + _attention_specs(n_groups, group_block, S),
        out_specs=pl.BlockSpec((1, T, n_groups * LANES), lambda b, i: (b, i, 0)),
        out_shape=jax.ShapeDtypeStruct((B, S, n_groups * LANES), BF16),
        scratch_shapes=[
            pltpu.VMEM((2, n_groups, T, 2 * T), F32),
            pltpu.VMEM((n_groups, 1, 2 * T), F32),
            pltpu.VMEM((n_groups, LANES, 2 * T), F32),
        ],
        compiler_params=pltpu.CompilerParams(
            dimension_semantics=("arbitrary", "arbitrary"), vmem_limit_bytes=VMEM_LIMIT),
        name="sb_attn",
    )(nu, qbd, k, vt)


def _mix_kernel(x_ref, yd_ref, ys_ref, g_ref, wgd_ref, wgs_ref, wbd_ref, wbs_ref, wo_ref, o_ref):
    x = x_ref[...]
    h = _rms(x, g_ref[...]).astype(BF16)
    merged = (jax.nn.sigmoid(_dot(h, wgd_ref[...])) * _dot(yd_ref[...], wbd_ref[...])
              + jax.nn.sigmoid(_dot(h, wgs_ref[...])) * _dot(ys_ref[...], wbs_ref[...]))
    o_ref[...] = x + _dot(merged.astype(BF16), wo_ref[...])


def _mix(x2d, yd, ys, g, wgd, wgs, wbd, wbs, wo, tm):
    N, D = x2d.shape
    const = lambda a: pl.BlockSpec(a.shape, lambda i: (0, 0))
    row = lambda w: pl.BlockSpec((tm, w), lambda i: (i, 0))
    return pl.pallas_call(
        _mix_kernel,
        grid=(N // tm,),
        in_specs=[row(D), row(yd.shape[1]), row(ys.shape[1]),
                  const(g), const(wgd), const(wgs), const(wbd), const(wbs), const(wo)],
        out_specs=row(D),
        out_shape=jax.ShapeDtypeStruct((N, D), F32),
        compiler_params=pltpu.CompilerParams(dimension_semantics=("arbitrary",), vmem_limit_bytes=VMEM_LIMIT),
        name="mix",
    )(x2d, yd, ys, g, wgd, wgs, wbd, wbs, wo)


def _ffn_kernel(x_ref, g_ref, wup_ref, cw_ref, cb_ref, wdn_ref, gf_ref, o_ref, abuf_ref, *, final_norm):
    tm = x_ref.shape[1]
    F = cw_ref.shape[1]
    H = SUBLANES

    @pl.when(pl.program_id(1) == 0)
    def _():
        abuf_ref[0:H] = jnp.zeros((H, F), F32)

    x = x_ref[0]
    h = _rms(x, g_ref[...]).astype(BF16)
    up = _dot(h, wup_ref[...])
    a = up[:, :F]
    abuf_ref[H:H + tm] = a
    conv = (cw_ref[0:1] * abuf_ref[H - 2:H - 2 + tm] + cw_ref[1:2] * abuf_ref[H - 1:H - 1 + tm]
            + cw_ref[2:3] * a + cb_ref[...])
    abuf_ref[0:H] = abuf_ref[tm:tm + H]
    gelu = 0.5 * conv * (1.0 + lax.erf(conv * (2.0 ** -0.5)))
    act = gelu * up[:, F:]
    x = x + _dot(act.astype(BF16), wdn_ref[...])
    o_ref[0] = _rms(x, gf_ref[...]) if final_norm else x


def _ffn(x, g, wup, cw, cb, wdn, gf, tm, final_norm):
    B, S, D = x.shape
    F = cw.shape[1]
    const = lambda a: pl.BlockSpec(a.shape, lambda b, i: (0, 0), pipeline_mode=pl.Buffered(1))
    return pl.pallas_call(
        functools.partial(_ffn_kernel, final_norm=final_norm),
        grid=(B, S // tm),
        in_specs=[pl.BlockSpec((1, tm, D), lambda b, i: (b, i, 0)),
                  const(g), const(wup), const(cw), const(cb), const(wdn), const(gf)],
        out_specs=pl.BlockSpec((1, tm, D), lambda b, i: (b, i, 0)),
        out_shape=jax.ShapeDtypeStruct((B, S, D), F32),
        scratch_shapes=[pltpu.VMEM((tm + SUBLANES, F), F32)],
        compiler_params=pltpu.CompilerParams(
            dimension_semantics=("arbitrary", "arbitrary"), vmem_limit_bytes=VMEM_LIMIT),
        name="ffn",
    )(x, g, wup, cw, cb, wdn, gf)


def _t5_bucket(rel):
    half = N_BUCKETS // 2
    ret = jnp.where(rel > 0, half, 0)
    n = jnp.abs(rel)
    max_exact = half // 2
    nf = jnp.maximum(n, 1).astype(jnp.float32)
    large = max_exact + (jnp.log(nf / max_exact) / math.log(MAX_DISTANCE / max_exact)
                         * (half - max_exact)).astype(jnp.int32)
    large = jnp.minimum(large, half - 1)
    return ret + jnp.where(n < max_exact, n, large)


def _near_bias_tables(rel_bias):
    T = ATT_TILE
    L = 2 * T
    H = rel_bias.shape[1]
    far = rel_bias[_t5_bucket(jnp.asarray(-MAX_DISTANCE))].astype(F32)
    m = jnp.arange(L)
    rel = jnp.where(m < T, -m, L - m)

    def toeplitz(rels):
        v = ((rel_bias[_t5_bucket(rels)].astype(F32) - far) * LOG2E).T
        rows = jnp.tile(v, (1, T))[:, :T * (L - 1)].reshape(H, T, L - 1)
        return rows[:, :, :T]

    kpos = jnp.arange(T)[:, None]
    qpos = jnp.arange(T)[None, :]
    visible = (kpos // CHUNK) <= (qpos // CHUNK)
    return jnp.where(visible[None], toeplitz(rel), -jnp.inf), toeplitz(rel - T)


def kernel(x, norm_mix_g, w_in, diff_lambda_q1, diff_lambda_k1, diff_lambda_q2, diff_lambda_k2, diff_subln_g,
           rel_bias, w_branch_diff, w_branch_sb, w_out, norm_ffn_g, w_ffn_up, ffn_conv_w, ffn_conv_b, w_ffn_down,
           norm_final_g):
    B, S, D = x.shape
    depth = w_in.shape[0]
    T = ATT_TILE
    n_heads_diff = rel_bias.shape[1]
    qk_w = n_heads_diff * 2 * HEAD_DIM
    ngd = qk_w // LANES
    assert S % T == 0 and T % CHUNK == 0 and T >= MAX_DISTANCE
    assert (6 * qk_w) % D == 0
    d_ff = ffn_conv_w.shape[-1]

    near_diag, near_prev = _near_bias_tables(rel_bias)
    neg_upper = -(jnp.arange(T)[None, :] >= jnp.arange(T)[:, None]).astype(BF16)

    for layer in range(depth):
        lambda_init = 0.8 - 0.6 * math.exp(-0.3 * layer)
        w = w_in[layer]
        cols = lambda i: w[:, i * qk_w:(i + 1) * qk_w]
        wk = jnp.concatenate([cols(1), cols(4)], axis=1).astype(BF16)
        wqvt = jnp.concatenate([cols(0), cols(3), cols(2), cols(5)], axis=1).T.astype(BF16)

        gate_col = 6 * qk_w // D
        wdn_flat = w_ffn_down[layer].reshape(-1, 2 * d_ff)
        cast_weights = [(w, D, gate_col), (w, D, gate_col + 1), (w_branch_diff[layer], D, 0),
                        (w_branch_sb[layer], D, 0), (w_out[layer], D, 0), (w_ffn_up[layer], 2 * d_ff, 0),
                        (wdn_flat, 2 * d_ff, 0)]
        (k, qbd, vt), (wgd, wgs, wbd, wbs, wo, wup, wdn) = _proj(
            x, norm_mix_g[layer][None], wk, wqvt, 2 * ngd, tiles_per_step=2, cast_weights=cast_weights)
        wdn = wdn.reshape(d_ff, D)

        row = lambda a: a[layer][None].astype(F32)
        y_diff = _diff_attention(qbd, k, vt, near_diag, near_prev, row(diff_lambda_q1), row(diff_lambda_k1),
                                 row(diff_lambda_q2), row(diff_lambda_k2), row(diff_subln_g), lambda_init, ngd, 0)
        y_sb = _sb_attention(qbd, k, vt, neg_upper, ngd, 1)

        x = _mix(x.reshape(B * S, D), y_diff.reshape(B * S, -1), y_sb.reshape(B * S, -1), norm_mix_g[layer][None],
                 wgd, wgs, wbd, wbs, wo, tm=512).reshape(B, S, D)
        x = _ffn(x, norm_ffn_g[layer][None], wup, ffn_conv_w[layer], ffn_conv_b[layer][None], wdn,
                 norm_final_g[None], tm=512, final_norm=layer == depth - 1)
    return x
```

```python
import functools
import math

import jax
import jax.numpy as jnp
from jax import lax
from jax.experimental import pallas as pl
from jax.experimental.pallas import tpu as pltpu

F32 = jnp.float32
BF16 = jnp.bfloat16

HEAD_DIM = 64
CHUNK = 64
N_BUCKETS = 32
MAX_DISTANCE = 128
NORM_EPS = 1e-6
LOG2E = math.log2(math.e)
DEAD_LOG2 = -150.0

LANES = 128
SUBLANES = 8
BF16_ROWS = 16
ATT_TILE = 256
VMEM_LIMIT = 56 * 1024 * 1024


def _dot(a, b):
    return jnp.dot(a, b, preferred_element_type=F32)


def _rms(x, g):
    return x * lax.rsqrt(jnp.mean(x * x, axis=-1, keepdims=True) + NORM_EPS) * g


def _proj_kernel(x_ref, g_ref, wk_ref, wqvt_ref, *refs, n_groups, scale, n_cast):
    cast_in, (k_ref, qbd_ref, vt_ref), cast_out = refs[:n_cast], refs[n_cast:n_cast + 3], refs[n_cast + 3:]
    T = ATT_TILE
    tiles = x_ref.shape[1] // T
    h = _rms(x_ref[0], g_ref[...]).astype(BF16)
    kk = _dot(h, wk_ref[...]).astype(BF16)
    for g in range(n_groups):
        k_ref[0, g] = kk[:, g * LANES:(g + 1) * LANES]
    qv = lax.dot_general(wqvt_ref[...], h, (((1,), (1,)), ((), ())), preferred_element_type=F32)
    nq = n_groups * LANES
    first = lax.broadcasted_iota(jnp.int32, (LANES, T), 0) < HEAD_DIM
    zero = jnp.zeros((LANES, T), BF16)
    for t in range(tiles):
        for g in range(n_groups):
            qg = (qv[g * LANES:(g + 1) * LANES, t * T:(t + 1) * T] * scale).astype(BF16)
            qbd_ref[0, t, g] = jnp.concatenate([jnp.where(first, qg, zero), jnp.where(first, zero, qg)], axis=1)
            vt_ref[0, t, g] = qv[nq + g * LANES:nq + (g + 1) * LANES, t * T:(t + 1) * T].astype(BF16)
    for src, dst in zip(cast_in, cast_out):
        dst[...] = src[...].astype(BF16)


def _proj(x, g, wk, wqvt, n_groups, tiles_per_step, layer, cast_weights):
    B, S, D = x.shape
    T = ATT_TILE
    nt = S // T
    tps = tiles_per_step
    steps_per_batch = nt // tps
    n_steps = B * steps_per_batch
    cast_in_specs, cast_out_specs, cast_out_shapes = [], [], []
    for w, width, col, hold in cast_weights:
        n_rows = w.shape[1]
        rows = n_rows * hold // n_steps
        assert rows * n_steps == n_rows * hold and rows % BF16_ROWS == 0 and w.shape[2] % width == 0
        cast_in_specs.append(pl.BlockSpec(
            (None, rows, width), lambda b, i, col=col, hold=hold: (layer, (b * steps_per_batch + i) // hold, col)))
        cast_out_specs.append(pl.BlockSpec(
            (rows, width), lambda b, i, hold=hold: ((b * steps_per_batch + i) // hold, 0)))
        cast_out_shapes.append(jax.ShapeDtypeStruct((n_rows, width), BF16))
    kern = functools.partial(_proj_kernel, n_groups=n_groups, scale=HEAD_DIM ** -0.5 * LOG2E,
                             n_cast=len(cast_weights))
    outs = pl.pallas_call(
        kern,
        grid=(B, steps_per_batch),
        in_specs=[
            pl.BlockSpec((1, tps * T, D), lambda b, i: (b, i, 0)),
            pl.BlockSpec((1, D), lambda b, i: (0, 0)),
            pl.BlockSpec(wk.shape, lambda b, i: (0, 0)),
            pl.BlockSpec(wqvt.shape, lambda b, i: (0, 0)),
        ] + cast_in_specs,
        out_specs=[
            pl.BlockSpec((1, n_groups, tps * T, LANES), lambda b, i: (b, 0, i, 0)),
            pl.BlockSpec((1, tps, n_groups, LANES, 2 * T), lambda b, i: (b, i, 0, 0, 0)),
            pl.BlockSpec((1, tps, n_groups, LANES, T), lambda b, i: (b, i, 0, 0, 0)),
        ] + cast_out_specs,
        out_shape=[
            jax.ShapeDtypeStruct((B, n_groups, S, LANES), BF16),
            jax.ShapeDtypeStruct((B, nt, n_groups, LANES, 2 * T), BF16),
            jax.ShapeDtypeStruct((B, nt, n_groups, LANES, T), BF16),
        ] + cast_out_shapes,
        compiler_params=pltpu.CompilerParams(
            dimension_semantics=("arbitrary", "arbitrary"), vmem_limit_bytes=VMEM_LIMIT),
        name="proj",
    )(x, g, wk, wqvt, *[w for w, _, _, _ in cast_weights])
    return outs[:3], outs[3:]


DIFF_BATCH_BLOCK = 2
SB_BATCH_BLOCK = 1


def _units(k_ref, n_groups):
    return [(bb, g) for bb in range(k_ref.shape[0]) for g in range(n_groups)]


def _key_tile(k_ref, bb, g, j):
    T = ATT_TILE
    start = j * T if isinstance(j, int) else pl.multiple_of(j * T, T)
    return k_ref[bb, g, pl.ds(start, T), :]


def _attention_specs(n_groups, group_block, S, nb):
    T = ATT_TILE
    gb = group_block
    return [
        pl.BlockSpec((nb, 1, n_groups, LANES, 2 * T), lambda b, i: (b, i, gb, 0, 0)),
        pl.BlockSpec((nb, n_groups, S, LANES), lambda b, i: (b, gb, 0, 0)),
        pl.BlockSpec((nb, S // T, n_groups, LANES, T), lambda b, i: (b, 0, gb, 0, 0)),
    ]


def _diff_kernel(lq1_ref, lk1_ref, lq2_ref, lk2_ref, gs_ref, nbd_ref, nbp_ref, qbd_ref, k_ref, vt_ref,
                 y_ref, z_ref, mx_ref, l_ref, ot_ref, *, n_groups, lambda_init):
    T = ATT_TILE
    qi = pl.program_id(1)
    units = list(enumerate(_units(k_ref, n_groups)))

    def score_unit(u, bb, g, j, slot, nb_ref):
        st = _dot(_key_tile(k_ref, bb, g, j), qbd_ref[bb, 0, g])
        if nb_ref is not None:
            nb = nb_ref[g]
            st = jnp.concatenate([st[:, :T] + nb, st[:, T:] + nb], axis=1)
        z_ref[slot, u] = st

    def tile(j, slot, first=False, next_bias=None, prefetch=True):
        ps, alphas = [], []
        for u, _ in units:
            s = z_ref[slot, u]
            m = jnp.max(s, axis=0, keepdims=True)
            if not first:
                m_old = mx_ref[u]
                m = jnp.maximum(m_old, m)
                alphas.append(jnp.exp2(m_old - m))
            p = jnp.exp2(s - m)
            lsum = jnp.sum(p, axis=0, keepdims=True)
            l_ref[u] = lsum if first else alphas[u] * l_ref[u] + lsum
            mx_ref[u] = m
            ps.append(p.astype(BF16))
        for u, (bb, g) in units:
            if prefetch:
                score_unit(u, bb, g, jnp.maximum(j - 1, 0), 1 - slot, next_bias)
            pv = _dot(vt_ref[bb, j, g], ps[u])
            ot_ref[u] = pv if first else alphas[u] * ot_ref[u] + pv

    for u, (bb, g) in units:
        score_unit(u, bb, g, qi, 0, nbd_ref)
    tile(qi, 0, first=True, next_bias=nbp_ref)

    @pl.when(qi > 0)
    def _():
        tile(qi - 1, 1)

    rest = jnp.maximum(qi - 1, 0)

    def pair(t, carry):
        j = qi - 2 - 2 * t
        tile(j, 0)
        tile(j - 1, 1)
        return carry
    lax.fori_loop(0, lax.shift_right_logical(rest, 1), pair, 0)

    @pl.when(jnp.bitwise_and(rest, 1) == 1)
    def _():
        tile(0, 0, prefetch=False)

    lam = (jnp.exp(jnp.sum(lq1_ref[...] * lk1_ref[...], keepdims=True))
           - jnp.exp(jnp.sum(lq2_ref[...] * lk2_ref[...], keepdims=True)) + lambda_init)
    for u, (bb, g) in units:
        ot = ot_ref[u] * (1.0 / l_ref[u])
        o = ot[:, :T] - lam * ot[:, T:]
        o = o * lax.rsqrt(jnp.mean(o * o, axis=0, keepdims=True) + NORM_EPS)
        y = o.T * gs_ref[...] * (1.0 - lambda_init)
        y_ref[bb, :, g * LANES:(g + 1) * LANES] = y.astype(y_ref.dtype)


def _diff_attention(qbd, k, vt, near_diag, near_prev, lq1, lk1, lq2, lk2, gs, lambda_init, n_groups, group_block):
    B, _, S, _ = k.shape
    T = ATT_TILE
    nt = S // T
    nb = DIFF_BATCH_BLOCK
    n_units = nb * n_groups
    kern = functools.partial(_diff_kernel, n_groups=n_groups, lambda_init=lambda_init)
    small = lambda a: pl.BlockSpec(a.shape, lambda b, i: (0,) * a.ndim)
    return pl.pallas_call(
        kern,
        grid=(B // nb, nt),
        in_specs=[small(lq1), small(lk1), small(lq2), small(lk2), small(gs), small(near_diag), small(near_prev)]
        + _attention_specs(n_groups, group_block, S, nb),
        out_specs=pl.BlockSpec((nb, T, n_groups * LANES), lambda b, i: (b, i, 0)),
        out_shape=jax.ShapeDtypeStruct((B, S, n_groups * LANES), BF16),
        scratch_shapes=[
            pltpu.VMEM((2, n_units, T, 2 * T), F32),
            pltpu.VMEM((n_units, 1, 2 * T), F32),
            pltpu.VMEM((n_units, 1, 2 * T), F32),
            pltpu.VMEM((n_units, LANES, 2 * T), F32),
        ],
        compiler_params=pltpu.CompilerParams(
            dimension_semantics=("arbitrary", "arbitrary"), vmem_limit_bytes=VMEM_LIMIT),
        name="diff_attn",
    )(lq1, lk1, lq2, lk2, gs, near_diag, near_prev, qbd, k, vt)


def _sb_kernel(nu_ref, qbd_ref, k_ref, vt_ref, y_ref, z_ref, c_ref, ot_ref, *, n_groups):
    T = ATT_TILE
    qi = pl.program_id(1)
    units = list(enumerate(_units(k_ref, n_groups)))
    kk = lax.broadcasted_iota(jnp.int32, (T, 2 * T), 0)
    qq = lax.broadcasted_iota(jnp.int32, (T, 2 * T), 1)
    causal = kk < jnp.where(qq >= T, qq - T, qq)
    sign = jnp.uint32(0x80000000)

    c_ref[...] = jnp.zeros_like(c_ref)
    ot_ref[...] = jnp.zeros_like(ot_ref)

    def score_unit(u, bb, g, j, slot):
        z_ref[slot, u] = _dot(_key_tile(k_ref, bb, g, j), qbd_ref[bb, 0, g])

    def score_tile(j, slot):
        for u, (bb, g) in units:
            score_unit(u, bb, g, j, slot)

    def tile(j, slot, masked, prefetch):
        sps = []
        for u, _ in units:
            z = z_ref[slot, u]
            neg_abs = lax.bitcast_convert_type(lax.bitcast_convert_type(z, jnp.uint32) | sign, F32)
            sp = jnp.maximum(z, 0.0) + jnp.log2(1.0 + jnp.exp2(neg_abs))
            if masked:
                sp = jnp.where(causal, sp, 0.0)
            sps.append(sp.astype(BF16))
        incls = []
        for u, (bb, g) in units:
            if prefetch:
                score_unit(u, bb, g, jnp.maximum(j - 1, 0), 1 - slot)
            incls.append(_dot(nu_ref[...], sps[u]))
        ws = []
        for u, _ in units:
            w = jnp.exp2((z_ref[slot, u] + c_ref[u]) + incls[u])
            if masked:
                w = jnp.where(causal, w, 0.0)
            ws.append(w.astype(BF16))
            c_ref[u] += incls[u][0:1]
        for u, (bb, g) in units:
            ot_ref[u] += _dot(vt_ref[bb, j, g], ws[u])

    score_tile(qi, 0)
    tile(qi, 0, True, prefetch=True)

    @pl.when(qi > 0)
    def _():
        tile(qi - 1, 1, False, prefetch=False)

    def alive():
        return jnp.max(c_ref[...]) >= DEAD_LOG2

    rest = jnp.maximum(qi - 1, 0)
    n_pairs = lax.shift_right_logical(rest, 1)

    def pair(carry):
        t, _ = carry
        j = qi - 2 - 2 * t
        score_tile(j, 0)
        tile(j, 0, False, prefetch=True)
        tile(j - 1, 1, False, prefetch=False)
        return t + 1, alive()
    _, go = lax.while_loop(lambda carry: jnp.logical_and(carry[0] < n_pairs, carry[1]), pair,
                           (jnp.int32(0), alive()))

    @pl.when(jnp.logical_and(jnp.bitwise_and(rest, 1) == 1, go))
    def _():
        score_tile(0, 0)
        tile(0, 0, False, prefetch=False)

    for u, (bb, g) in units:
        ot = ot_ref[u]
        o = jnp.concatenate([ot[:HEAD_DIM, :T], ot[HEAD_DIM:, T:]], axis=0)
        y_ref[bb, :, g * LANES:(g + 1) * LANES] = o.T.astype(y_ref.dtype)


def _sb_attention(qbd, k, vt, nu, n_groups, group_block):
    B, _, S, _ = k.shape
    T = ATT_TILE
    nb = SB_BATCH_BLOCK
    n_units = nb * n_groups
    kern = functools.partial(_sb_kernel, n_groups=n_groups)
    return pl.pallas_call(
        kern,
        grid=(B // nb, S // T),
        in_specs=[pl.BlockSpec(nu.shape, lambda b, i: (0, 0))] + _attention_specs(n_groups, group_block, S, nb),
        out_specs=pl.BlockSpec((nb, T, n_groups * LANES), lambda b, i: (b, i, 0)),
        out_shape=jax.ShapeDtypeStruct((B, S, n_groups * LANES), BF16),
        scratch_shapes=[
            pltpu.VMEM((2, n_units, T, 2 * T), F32),
            pltpu.VMEM((n_units, 1, 2 * T), F32),
            pltpu.VMEM((n_units, LANES, 2 * T), F32),
        ],
        compiler_params=pltpu.CompilerParams(
            dimension_semantics=("arbitrary", "arbitrary"), vmem_limit_bytes=VMEM_LIMIT),
        name="sb_attn",
    )(nu, qbd, k, vt)


def _mix_kernel(x_ref, yd_ref, ys_ref, g_ref, wgd_ref, wgs_ref, wbd_ref, wbs_ref, wo_ref, o_ref):
    x = x_ref[...]
    h = _rms(x, g_ref[...]).astype(BF16)
    merged = (jax.nn.sigmoid(_dot(h, wgd_ref[...])) * _dot(yd_ref[...], wbd_ref[...])
              + jax.nn.sigmoid(_dot(h, wgs_ref[...])) * _dot(ys_ref[...], wbs_ref[...]))
    o_ref[...] = x + _dot(merged.astype(BF16), wo_ref[...])


def _mix(x2d, yd, ys, g, wgd, wgs, wbd, wbs, wo, tm):
    N, D = x2d.shape
    const = lambda a: pl.BlockSpec(a.shape, lambda i: (0, 0))
    row = lambda w: pl.BlockSpec((tm, w), lambda i: (i, 0))
    return pl.pallas_call(
        _mix_kernel,
        grid=(N // tm,),
        in_specs=[row(D), row(yd.shape[1]), row(ys.shape[1]),
                  const(g), const(wgd), const(wgs), const(wbd), const(wbs), const(wo)],
        out_specs=row(D),
        out_shape=jax.ShapeDtypeStruct((N, D), F32),
        compiler_params=pltpu.CompilerParams(dimension_semantics=("arbitrary",), vmem_limit_bytes=VMEM_LIMIT),
        name="mix",
    )(x2d, yd, ys, g, wgd, wgs, wbd, wbs, wo)


def _ffn_kernel(x_ref, g_ref, wup_ref, cw_ref, cb_ref, wdn_ref, gf_ref, o_ref, abuf_ref, *, final_norm):
    tm = x_ref.shape[1]
    F = cw_ref.shape[1]
    H = SUBLANES

    @pl.when(pl.program_id(1) == 0)
    def _():
        abuf_ref[0:H] = jnp.zeros((H, F), F32)

    x = x_ref[0]
    h = _rms(x, g_ref[...]).astype(BF16)
    up = _dot(h, wup_ref[...])
    a = up[:, :F]
    abuf_ref[H:H + tm] = a
    conv = (cw_ref[0:1] * abuf_ref[H - 2:H - 2 + tm] + cw_ref[1:2] * abuf_ref[H - 1:H - 1 + tm]
            + cw_ref[2:3] * a + cb_ref[...])
    abuf_ref[0:H] = abuf_ref[tm:tm + H]
    gelu = 0.5 * conv * (1.0 + lax.erf(conv * (2.0 ** -0.5)))
    act = gelu * up[:, F:]
    x = x + _dot(act.astype(BF16), wdn_ref[...])
    o_ref[0] = _rms(x, gf_ref[...]) if final_norm else x


def _ffn(x, g, wup, cw, cb, wdn, gf, tm, final_norm):
    B, S, D = x.shape
    F = cw.shape[1]
    const = lambda a: pl.BlockSpec(a.shape, lambda b, i: (0, 0), pipeline_mode=pl.Buffered(1))
    return pl.pallas_call(
        functools.partial(_ffn_kernel, final_norm=final_norm),
        grid=(B, S // tm),
        in_specs=[pl.BlockSpec((1, tm, D), lambda b, i: (b, i, 0)),
                  const(g), const(wup), const(cw), const(cb), const(wdn), const(gf)],
        out_specs=pl.BlockSpec((1, tm, D), lambda b, i: (b, i, 0)),
        out_shape=jax.ShapeDtypeStruct((B, S, D), F32),
        scratch_shapes=[pltpu.VMEM((tm + SUBLANES, F), F32)],
        compiler_params=pltpu.CompilerParams(
            dimension_semantics=("arbitrary", "arbitrary"), vmem_limit_bytes=VMEM_LIMIT),
        name="ffn",
    )(x, g, wup, cw, cb, wdn, gf)


def _t5_bucket(rel):
    half = N_BUCKETS // 2
    ret = jnp.where(rel > 0, half, 0)
    n = jnp.abs(rel)
    max_exact = half // 2
    nf = jnp.maximum(n, 1).astype(jnp.float32)
    large = max_exact + (jnp.log(nf / max_exact) / math.log(MAX_DISTANCE / max_exact)
                         * (half - max_exact)).astype(jnp.int32)
    large = jnp.minimum(large, half - 1)
    return ret + jnp.where(n < max_exact, n, large)


def _near_bias_tables(rel_bias):
    T = ATT_TILE
    L = 2 * T
    H = rel_bias.shape[1]
    far = rel_bias[_t5_bucket(jnp.asarray(-MAX_DISTANCE))].astype(F32)
    m = jnp.arange(L)
    rel = jnp.where(m < T, -m, L - m)

    def toeplitz(rels):
        v = ((rel_bias[_t5_bucket(rels)].astype(F32) - far) * LOG2E).T
        rows = jnp.tile(v, (1, T))[:, :T * (L - 1)].reshape(H, T, L - 1)
        return rows[:, :, :T]

    kpos = jnp.arange(T)[:, None]
    qpos = jnp.arange(T)[None, :]
    visible = (kpos // CHUNK) <= (qpos // CHUNK)
    return jnp.where(visible[None], toeplitz(rel), -jnp.inf), toeplitz(rel - T)


def kernel(x, norm_mix_g, w_in, diff_lambda_q1, diff_lambda_k1, diff_lambda_q2, diff_lambda_k2, diff_subln_g,
           rel_bias, w_branch_diff, w_branch_sb, w_out, norm_ffn_g, w_ffn_up, ffn_conv_w, ffn_conv_b, w_ffn_down,
           norm_final_g):
    B, S, D = x.shape
    depth = w_in.shape[0]
    T = ATT_TILE
    n_heads_diff = rel_bias.shape[1]
    qk_w = n_heads_diff * 2 * HEAD_DIM
    ngd = qk_w // LANES
    assert S % T == 0 and T % CHUNK == 0 and T >= MAX_DISTANCE and B % DIFF_BATCH_BLOCK == 0
    assert (6 * qk_w) % D == 0
    d_ff = ffn_conv_w.shape[-1]

    near_diag, near_prev = _near_bias_tables(rel_bias)
    neg_upper = -(jnp.arange(T)[None, :] >= jnp.arange(T)[:, None]).astype(BF16)

    for layer in range(depth):
        lambda_init = 0.8 - 0.6 * math.exp(-0.3 * layer)
        w = w_in[layer]
        cols = lambda i: w[:, i * qk_w:(i + 1) * qk_w]
        wk = jnp.concatenate([cols(1), cols(4)], axis=1).astype(BF16)
        wqvt = jnp.concatenate([cols(0), cols(3), cols(2), cols(5)], axis=1).T.astype(BF16)

        gate_col = 6 * qk_w // D
        cast_weights = [(w_in, D, gate_col, 1), (w_in, D, gate_col + 1, 1), (w_branch_diff, D, 0, 1),
                        (w_branch_sb, D, 0, 1), (w_out, D, 0, 1), (w_ffn_up, 2 * d_ff, 0, 1),
                        (w_ffn_down, D, 0, 2)]
        (k, qbd, vt), (wgd, wgs, wbd, wbs, wo, wup, wdn) = _proj(
            x, norm_mix_g[layer][None], wk, wqvt, 2 * ngd, tiles_per_step=2, layer=layer, cast_weights=cast_weights)

        row = lambda a: a[layer][None].astype(F32)
        y_diff = _diff_attention(qbd, k, vt, near_diag, near_prev, row(diff_lambda_q1), row(diff_lambda_k1),
                                 row(diff_lambda_q2), row(diff_lambda_k2), row(diff_subln_g), lambda_init, ngd, 0)
        y_sb = _sb_attention(qbd, k, vt, neg_upper, ngd, 1)

        x = _mix(x.reshape(B * S, D), y_diff.reshape(B * S, -1), y_sb.reshape(B * S, -1), norm_mix_g[layer][None],
                 wgd, wgs, wbd, wbs, wo, tm=512).reshape(B, S, D)
        x = _ffn(x, norm_ffn_g[layer][None], wup, ffn_conv_w[layer], ffn_conv_b[layer][None], wdn,
                 norm_final_g[None], tm=512, final_norm=layer == depth - 1)
    return x
```

```python
import functools
import math

import jax
import jax.numpy as jnp
from jax import lax
from jax.experimental import pallas as pl
from jax.experimental.pallas import tpu as pltpu

F32 = jnp.float32
BF16 = jnp.bfloat16

HEAD_DIM = 64
CHUNK = 64
N_BUCKETS = 32
MAX_DISTANCE = 128
NORM_EPS = 1e-6
LOG2E = math.log2(math.e)
DEAD_LOG2 = -150.0

LANES = 128
SUBLANES = 8
BF16_ROWS = 16
ATT_TILE = 256
VMEM_LIMIT = 56 * 1024 * 1024


def _dot(a, b):
    return jnp.dot(a, b, preferred_element_type=F32)


def _rms(x, g):
    return x * lax.rsqrt(jnp.mean(x * x, axis=-1, keepdims=True) + NORM_EPS) * g


def _proj_kernel(x_ref, g_ref, dq_ref, dk_ref, dv_ref, sq_ref, sk_ref, sv_ref, *refs, n_groups, scale, n_cast):
    cast_in, (k_ref, qbd_ref, vt_ref), cast_out = refs[:n_cast], refs[n_cast:n_cast + 3], refs[n_cast + 3:n_cast * 2 + 3]
    wk_ref, wqvt_ref = refs[n_cast * 2 + 3:]
    T = ATT_TILE
    tiles = x_ref.shape[1] // T

    @pl.when(jnp.logical_and(pl.program_id(0) == 0, pl.program_id(1) == 0))
    def _():
        width = dk_ref.shape[1]
        for n, ref in enumerate((dk_ref, sk_ref)):
            wk_ref[:, n * width:(n + 1) * width] = ref[...].astype(BF16)
        for n, ref in enumerate((dq_ref, sq_ref, dv_ref, sv_ref)):
            wqvt_ref[n * width:(n + 1) * width, :] = ref[...].T.astype(BF16)

    h = _rms(x_ref[0], g_ref[...]).astype(BF16)
    kk = _dot(h, wk_ref[...]).astype(BF16)
    for g in range(n_groups):
        k_ref[0, g] = kk[:, g * LANES:(g + 1) * LANES]
    qv = lax.dot_general(wqvt_ref[...], h, (((1,), (1,)), ((), ())), preferred_element_type=F32)
    nq = n_groups * LANES
    first = lax.broadcasted_iota(jnp.int32, (LANES, T), 0) < HEAD_DIM
    zero = jnp.zeros((LANES, T), BF16)
    for t in range(tiles):
        for g in range(n_groups):
            qg = (qv[g * LANES:(g + 1) * LANES, t * T:(t + 1) * T] * scale).astype(BF16)
            qbd_ref[0, t, g] = jnp.concatenate([jnp.where(first, qg, zero), jnp.where(first, zero, qg)], axis=1)
            vt_ref[0, t, g] = qv[nq + g * LANES:nq + (g + 1) * LANES, t * T:(t + 1) * T].astype(BF16)
    for src, dst in zip(cast_in, cast_out):
        dst[...] = src[...].astype(BF16)


def _proj(x, g, w_in, qk_w, tiles_per_step, layer, cast_weights):
    B, S, D = x.shape
    T = ATT_TILE
    nt = S // T
    tps = tiles_per_step
    steps_per_batch = nt // tps
    n_steps = B * steps_per_batch
    n_groups = 2 * qk_w // LANES
    w_block = lambda col: pl.BlockSpec((None, D, qk_w), lambda b, i: (layer, 0, col), pipeline_mode=pl.Buffered(1))
    cast_in_specs, cast_out_specs, cast_out_shapes = [], [], []
    for w, width, col, hold in cast_weights:
        n_rows = w.shape[1]
        rows = n_rows * hold // n_steps
        assert rows * n_steps == n_rows * hold and rows % BF16_ROWS == 0 and w.shape[2] % width == 0
        cast_in_specs.append(pl.BlockSpec(
            (None, rows, width), lambda b, i, col=col, hold=hold: (layer, (b * steps_per_batch + i) // hold, col)))
        cast_out_specs.append(pl.BlockSpec(
            (rows, width), lambda b, i, hold=hold: ((b * steps_per_batch + i) // hold, 0)))
        cast_out_shapes.append(jax.ShapeDtypeStruct((n_rows, width), BF16))
    kern = functools.partial(_proj_kernel, n_groups=n_groups, scale=HEAD_DIM ** -0.5 * LOG2E,
                             n_cast=len(cast_weights))
    outs = pl.pallas_call(
        kern,
        grid=(B, steps_per_batch),
        in_specs=[
            pl.BlockSpec((1, tps * T, D), lambda b, i: (b, i, 0)),
            pl.BlockSpec((1, D), lambda b, i: (0, 0)),
        ] + [w_block(col) for col in range(6)] + cast_in_specs,
        out_specs=[
            pl.BlockSpec((1, n_groups, tps * T, LANES), lambda b, i: (b, 0, i, 0)),
            pl.BlockSpec((1, tps, n_groups, LANES, 2 * T), lambda b, i: (b, i, 0, 0, 0)),
            pl.BlockSpec((1, tps, n_groups, LANES, T), lambda b, i: (b, i, 0, 0, 0)),
        ] + cast_out_specs,
        out_shape=[
            jax.ShapeDtypeStruct((B, n_groups, S, LANES), BF16),
            jax.ShapeDtypeStruct((B, nt, n_groups, LANES, 2 * T), BF16),
            jax.ShapeDtypeStruct((B, nt, n_groups, LANES, T), BF16),
        ] + cast_out_shapes,
        scratch_shapes=[pltpu.VMEM((D, 2 * qk_w), BF16), pltpu.VMEM((4 * qk_w, D), BF16)],
        compiler_params=pltpu.CompilerParams(
            dimension_semantics=("arbitrary", "arbitrary"), vmem_limit_bytes=VMEM_LIMIT),
        name="proj",
    )(x, g, *([w_in] * 6), *[w for w, _, _, _ in cast_weights])
    return outs[:3], outs[3:]


DIFF_BATCH_BLOCK = 2
SB_BATCH_BLOCK = 1


def _units(k_ref, n_groups):
    return [(bb, g) for bb in range(k_ref.shape[0]) for g in range(n_groups)]


def _key_tile(k_ref, bb, g, j):
    T = ATT_TILE
    start = j * T if isinstance(j, int) else pl.multiple_of(j * T, T)
    return k_ref[bb, g, pl.ds(start, T), :]


def _attention_specs(n_groups, group_block, S, nb):
    T = ATT_TILE
    gb = group_block
    return [
        pl.BlockSpec((nb, 1, n_groups, LANES, 2 * T), lambda b, i: (b, i, gb, 0, 0)),
        pl.BlockSpec((nb, n_groups, S, LANES), lambda b, i: (b, gb, 0, 0)),
        pl.BlockSpec((nb, S // T, n_groups, LANES, T), lambda b, i: (b, 0, gb, 0, 0)),
    ]


def _diff_kernel(lq1_ref, lk1_ref, lq2_ref, lk2_ref, gs_ref, nbd_ref, nbp_ref, qbd_ref, k_ref, vt_ref,
                 y_ref, z_ref, mx_ref, l_ref, ot_ref, *, n_groups, lambda_init):
    T = ATT_TILE
    qi = pl.program_id(1)
    units = list(enumerate(_units(k_ref, n_groups)))

    def score_unit(u, bb, g, j, slot, nb_ref):
        st = _dot(_key_tile(k_ref, bb, g, j), qbd_ref[bb, 0, g])
        if nb_ref is not None:
            nb = nb_ref[g]
            st = jnp.concatenate([st[:, :T] + nb, st[:, T:] + nb], axis=1)
        z_ref[slot, u] = st

    def tile(j, slot, first=False, next_bias=None, prefetch=True):
        ps, alphas = [], []
        for u, _ in units:
            s = z_ref[slot, u]
            m = jnp.max(s, axis=0, keepdims=True)
            if not first:
                m_old = mx_ref[u]
                m = jnp.maximum(m_old, m)
                alphas.append(jnp.exp2(m_old - m))
            p = jnp.exp2(s - m)
            lsum = jnp.sum(p, axis=0, keepdims=True)
            l_ref[u] = lsum if first else alphas[u] * l_ref[u] + lsum
            mx_ref[u] = m
            ps.append(p.astype(BF16))
        for u, (bb, g) in units:
            if prefetch:
                score_unit(u, bb, g, jnp.maximum(j - 1, 0), 1 - slot, next_bias)
            pv = _dot(vt_ref[bb, j, g], ps[u])
            ot_ref[u] = pv if first else alphas[u] * ot_ref[u] + pv

    for u, (bb, g) in units:
        score_unit(u, bb, g, qi, 0, nbd_ref)
    tile(qi, 0, first=True, next_bias=nbp_ref)

    @pl.when(qi > 0)
    def _():
        tile(qi - 1, 1)

    rest = jnp.maximum(qi - 1, 0)

    def pair(t, carry):
        j = qi - 2 - 2 * t
        tile(j, 0)
        tile(j - 1, 1)
        return carry
    lax.fori_loop(0, lax.shift_right_logical(rest, 1), pair, 0)

    @pl.when(jnp.bitwise_and(rest, 1) == 1)
    def _():
        tile(0, 0, prefetch=False)

    lam = (jnp.exp(jnp.sum(lq1_ref[...] * lk1_ref[...], keepdims=True))
           - jnp.exp(jnp.sum(lq2_ref[...] * lk2_ref[...], keepdims=True)) + lambda_init)
    for u, (bb, g) in units:
        ot = ot_ref[u] * (1.0 / l_ref[u])
        o = ot[:, :T] - lam * ot[:, T:]
        o = o * lax.rsqrt(jnp.mean(o * o, axis=0, keepdims=True) + NORM_EPS)
        y = o.T * gs_ref[...] * (1.0 - lambda_init)
        y_ref[bb, :, g * LANES:(g + 1) * LANES] = y.astype(y_ref.dtype)


def _diff_attention(qbd, k, vt, near_diag, near_prev, lq1, lk1, lq2, lk2, gs, lambda_init, n_groups, group_block):
    B, _, S, _ = k.shape
    T = ATT_TILE
    nt = S // T
    nb = DIFF_BATCH_BLOCK
    n_units = nb * n_groups
    kern = functools.partial(_diff_kernel, n_groups=n_groups, lambda_init=lambda_init)
    small = lambda a: pl.BlockSpec(a.shape, lambda b, i: (0,) * a.ndim)
    return pl.pallas_call(
        kern,
        grid=(B // nb, nt),
        in_specs=[small(lq1), small(lk1), small(lq2), small(lk2), small(gs), small(near_diag), small(near_prev)]
        + _attention_specs(n_groups, group_block, S, nb),
        out_specs=pl.BlockSpec((nb, T, n_groups * LANES), lambda b, i: (b, i, 0)),
        out_shape=jax.ShapeDtypeStruct((B, S, n_groups * LANES), BF16),
        scratch_shapes=[
            pltpu.VMEM((2, n_units, T, 2 * T), F32),
            pltpu.VMEM((n_units, 1, 2 * T), F32),
            pltpu.VMEM((n_units, 1, 2 * T), F32),
            pltpu.VMEM((n_units, LANES, 2 * T), F32),
        ],
        compiler_params=pltpu.CompilerParams(
            dimension_semantics=("arbitrary", "arbitrary"), vmem_limit_bytes=VMEM_LIMIT),
        name="diff_attn",
    )(lq1, lk1, lq2, lk2, gs, near_diag, near_prev, qbd, k, vt)


def _sb_kernel(nu_ref, qbd_ref, k_ref, vt_ref, y_ref, z_ref, c_ref, ot_ref, *, n_groups):
    T = ATT_TILE
    qi = pl.program_id(1)
    units = list(enumerate(_units(k_ref, n_groups)))
    kk = lax.broadcasted_iota(jnp.int32, (T, 2 * T), 0)
    qq = lax.broadcasted_iota(jnp.int32, (T, 2 * T), 1)
    causal = kk < jnp.where(qq >= T, qq - T, qq)
    sign = jnp.uint32(0x80000000)

    c_ref[...] = jnp.zeros_like(c_ref)
    ot_ref[...] = jnp.zeros_like(ot_ref)

    def score_unit(u, bb, g, j, slot):
        z_ref[slot, u] = _dot(_key_tile(k_ref, bb, g, j), qbd_ref[bb, 0, g])

    def score_tile(j, slot):
        for u, (bb, g) in units:
            score_unit(u, bb, g, j, slot)

    def tile(j, slot, masked, prefetch):
        sps = []
        for u, _ in units:
            z = z_ref[slot, u]
            neg_abs = lax.bitcast_convert_type(lax.bitcast_convert_type(z, jnp.uint32) | sign, F32)
            sp = jnp.maximum(z, 0.0) + jnp.log2(1.0 + jnp.exp2(neg_abs))
            if masked:
                sp = jnp.where(causal, sp, 0.0)
            sps.append(sp.astype(BF16))
        incls = []
        for u, (bb, g) in units:
            if prefetch:
                score_unit(u, bb, g, jnp.maximum(j - 1, 0), 1 - slot)
            incls.append(_dot(nu_ref[...], sps[u]))
        ws = []
        for u, _ in units:
            w = jnp.exp2((z_ref[slot, u] + c_ref[u]) + incls[u])
            if masked:
                w = jnp.where(causal, w, 0.0)
            ws.append(w.astype(BF16))
            c_ref[u] += incls[u][0:1]
        for u, (bb, g) in units:
            ot_ref[u] += _dot(vt_ref[bb, j, g], ws[u])

    score_tile(qi, 0)
    tile(qi, 0, True, prefetch=True)

    @pl.when(qi > 0)
    def _():
        tile(qi - 1, 1, False, prefetch=False)

    def alive():
        return jnp.max(c_ref[...]) >= DEAD_LOG2

    rest = jnp.maximum(qi - 1, 0)
    n_pairs = lax.shift_right_logical(rest, 1)

    def pair(carry):
        t, _ = carry
        j = qi - 2 - 2 * t
        score_tile(j, 0)
        tile(j, 0, False, prefetch=True)
        tile(j - 1, 1, False, prefetch=False)
        return t + 1, alive()
    _, go = lax.while_loop(lambda carry: jnp.logical_and(carry[0] < n_pairs, carry[1]), pair,
                           (jnp.int32(0), alive()))

    @pl.when(jnp.logical_and(jnp.bitwise_and(rest, 1) == 1, go))
    def _():
        score_tile(0, 0)
        tile(0, 0, False, prefetch=False)

    for u, (bb, g) in units:
        ot = ot_ref[u]
        o = jnp.concatenate([ot[:HEAD_DIM, :T], ot[HEAD_DIM:, T:]], axis=0)
        y_ref[bb, :, g * LANES:(g + 1) * LANES] = o.T.astype(y_ref.dtype)


def _sb_attention(qbd, k, vt, nu, n_groups, group_block):
    B, _, S, _ = k.shape
    T = ATT_TILE
    nb = SB_BATCH_BLOCK
    n_units = nb * n_groups
    kern = functools.partial(_sb_kernel, n_groups=n_groups)
    return pl.pallas_call(
        kern,
        grid=(B // nb, S // T),
        in_specs=[pl.BlockSpec(nu.shape, lambda b, i: (0, 0))] + _attention_specs(n_groups, group_block, S, nb),
        out_specs=pl.BlockSpec((nb, T, n_groups * LANES), lambda b, i: (b, i, 0)),
        out_shape=jax.ShapeDtypeStruct((B, S, n_groups * LANES), BF16),
        scratch_shapes=[
            pltpu.VMEM((2, n_units, T, 2 * T), F32),
            pltpu.VMEM((n_units, 1, 2 * T), F32),
            pltpu.VMEM((n_units, LANES, 2 * T), F32),
        ],
        compiler_params=pltpu.CompilerParams(
            dimension_semantics=("arbitrary", "arbitrary"), vmem_limit_bytes=VMEM_LIMIT),
        name="sb_attn",
    )(nu, qbd, k, vt)


def _mix_kernel(x_ref, yd_ref, ys_ref, g_ref, wgd_ref, wgs_ref, wbd_ref, wbs_ref, wo_ref, o_ref):
    x = x_ref[...]
    h = _rms(x, g_ref[...]).astype(BF16)
    merged = (jax.nn.sigmoid(_dot(h, wgd_ref[...])) * _dot(yd_ref[...], wbd_ref[...])
              + jax.nn.sigmoid(_dot(h, wgs_ref[...])) * _dot(ys_ref[...], wbs_ref[...]))
    o_ref[...] = x + _dot(merged.astype(BF16), wo_ref[...])


def _mix(x2d, yd, ys, g, wgd, wgs, wbd, wbs, wo, tm):
    N, D = x2d.shape
    const = lambda a: pl.BlockSpec(a.shape, lambda i: (0, 0))
    row = lambda w: pl.BlockSpec((tm, w), lambda i: (i, 0))
    return pl.pallas_call(
        _mix_kernel,
        grid=(N // tm,),
        in_specs=[row(D), row(yd.shape[1]), row(ys.shape[1]),
                  const(g), const(wgd), const(wgs), const(wbd), const(wbs), const(wo)],
        out_specs=row(D),
        out_shape=jax.ShapeDtypeStruct((N, D), F32),
        compiler_params=pltpu.CompilerParams(dimension_semantics=("arbitrary",), vmem_limit_bytes=VMEM_LIMIT),
        name="mix",
    )(x2d, yd, ys, g, wgd, wgs, wbd, wbs, wo)


def _ffn_kernel(x_ref, g_ref, wup_ref, cw_ref, cb_ref, wdn_ref, gf_ref, o_ref, abuf_ref, *, final_norm):
    tm = x_ref.shape[1]
    F = cw_ref.shape[1]
    H = SUBLANES

    @pl.when(pl.program_id(1) == 0)
    def _():
        abuf_ref[0:H] = jnp.zeros((H, F), F32)

    x = x_ref[0]
    h = _rms(x, g_ref[...]).astype(BF16)
    up = _dot(h, wup_ref[...])
    a = up[:, :F]
    abuf_ref[H:H + tm] = a
    conv = (cw_ref[0:1] * abuf_ref[H - 2:H - 2 + tm] + cw_ref[1:2] * abuf_ref[H - 1:H - 1 + tm]
            + cw_ref[2:3] * a + cb_ref[...])
    abuf_ref[0:H] = abuf_ref[tm:tm + H]
    gelu = 0.5 * conv * (1.0 + lax.erf(conv * (2.0 ** -0.5)))
    act = gelu * up[:, F:]
    x = x + _dot(act.astype(BF16), wdn_ref[...])
    o_ref[0] = _rms(x, gf_ref[...]) if final_norm else x


def _ffn(x, g, wup, cw, cb, wdn, gf, tm, final_norm):
    B, S, D = x.shape
    F = cw.shape[1]
    const = lambda a: pl.BlockSpec(a.shape, lambda b, i: (0, 0), pipeline_mode=pl.Buffered(1))
    return pl.pallas_call(
        functools.partial(_ffn_kernel, final_norm=final_norm),
        grid=(B, S // tm),
        in_specs=[pl.BlockSpec((1, tm, D), lambda b, i: (b, i, 0)),
                  const(g), const(wup), const(cw), const(cb), const(wdn), const(gf)],
        out_specs=pl.BlockSpec((1, tm, D), lambda b, i: (b, i, 0)),
        out_shape=jax.ShapeDtypeStruct((B, S, D), F32),
        scratch_shapes=[pltpu.VMEM((tm + SUBLANES, F), F32)],
        compiler_params=pltpu.CompilerParams(
            dimension_semantics=("arbitrary", "arbitrary"), vmem_limit_bytes=VMEM_LIMIT),
        name="ffn",
    )(x, g, wup, cw, cb, wdn, gf)


def _t5_bucket(rel):
    half = N_BUCKETS // 2
    ret = jnp.where(rel > 0, half, 0)
    n = jnp.abs(rel)
    max_exact = half // 2
    nf = jnp.maximum(n, 1).astype(jnp.float32)
    large = max_exact + (jnp.log(nf / max_exact) / math.log(MAX_DISTANCE / max_exact)
                         * (half - max_exact)).astype(jnp.int32)
    large = jnp.minimum(large, half - 1)
    return ret + jnp.where(n < max_exact, n, large)


def _near_bias_tables(rel_bias):
    T = ATT_TILE
    L = 2 * T
    H = rel_bias.shape[1]
    far = rel_bias[_t5_bucket(jnp.asarray(-MAX_DISTANCE))].astype(F32)
    m = jnp.arange(L)
    rel = jnp.where(m < T, -m, L - m)

    def toeplitz(rels):
        v = ((rel_bias[_t5_bucket(rels)].astype(F32) - far) * LOG2E).T
        rows = jnp.tile(v, (1, T))[:, :T * (L - 1)].reshape(H, T, L - 1)
        return rows[:, :, :T]

    kpos = jnp.arange(T)[:, None]
    qpos = jnp.arange(T)[None, :]
    visible = (kpos // CHUNK) <= (qpos // CHUNK)
    return jnp.where(visible[None], toeplitz(rel), -jnp.inf), toeplitz(rel - T)


def kernel(x, norm_mix_g, w_in, diff_lambda_q1, diff_lambda_k1, diff_lambda_q2, diff_lambda_k2, diff_subln_g,
           rel_bias, w_branch_diff, w_branch_sb, w_out, norm_ffn_g, w_ffn_up, ffn_conv_w, ffn_conv_b, w_ffn_down,
           norm_final_g):
    B, S, D = x.shape
    depth = w_in.shape[0]
    T = ATT_TILE
    n_heads_diff = rel_bias.shape[1]
    qk_w = n_heads_diff * 2 * HEAD_DIM
    ngd = qk_w // LANES
    assert S % T == 0 and T % CHUNK == 0 and T >= MAX_DISTANCE and B % DIFF_BATCH_BLOCK == 0
    assert (6 * qk_w) % D == 0
    d_ff = ffn_conv_w.shape[-1]

    near_diag, near_prev = _near_bias_tables(rel_bias)
    neg_upper = -(jnp.arange(T)[None, :] >= jnp.arange(T)[:, None]).astype(BF16)

    for layer in range(depth):
        lambda_init = 0.8 - 0.6 * math.exp(-0.3 * layer)
        gate_col = 6 * qk_w // D
        cast_weights = [(w_in, D, gate_col, 1), (w_in, D, gate_col + 1, 1), (w_branch_diff, D, 0, 1),
                        (w_branch_sb, D, 0, 1), (w_out, D, 0, 1), (w_ffn_up, 2 * d_ff, 0, 1),
                        (w_ffn_down, D, 0, 2)]
        (k, qbd, vt), (wgd, wgs, wbd, wbs, wo, wup, wdn) = _proj(
            x, norm_mix_g[layer][None], w_in, qk_w, tiles_per_step=2, layer=layer, cast_weights=cast_weights)

        row = lambda a: a[layer][None].astype(F32)
        y_diff = _diff_attention(qbd, k, vt, near_diag, near_prev, row(diff_lambda_q1), row(diff_lambda_k1),
                                 row(diff_lambda_q2), row(diff_lambda_k2), row(diff_subln_g), lambda_init, ngd, 0)
        y_sb = _sb_attention(qbd, k, vt, neg_upper, ngd, 1)

        x = _mix(x.reshape(B * S, D), y_diff.reshape(B * S, -1), y_sb.reshape(B * S, -1), norm_mix_g[layer][None],
                 wgd, wgs, wbd, wbs, wo, tm=512).reshape(B, S, D)
        x = _ffn(x, norm_ffn_g[layer][None], wup, ffn_conv_w[layer], ffn_conv_b[layer][None], wdn,
                 norm_final_g[None], tm=512, final_norm=layer == depth - 1)
    return x
```

```python
import functools
import math

import jax
import jax.numpy as jnp
from jax import lax
from jax.experimental import pallas as pl
from jax.experimental.pallas import tpu as pltpu

F32 = jnp.float32
BF16 = jnp.bfloat16

HEAD_DIM = 64
CHUNK = 64
N_BUCKETS = 32
MAX_DISTANCE = 128
NORM_EPS = 1e-6
LOG2E = math.log2(math.e)
DEAD_LOG2 = -150.0

LANES = 128
SUBLANES = 8
BF16_ROWS = 16
ATT_TILE = 256
VMEM_LIMIT = 56 * 1024 * 1024


def _dot(a, b):
    return jnp.dot(a, b, preferred_element_type=F32)


def _rms(x, g):
    return x * lax.rsqrt(jnp.mean(x * x, axis=-1, keepdims=True) + NORM_EPS) * g


def _proj_kernel(x_ref, g_ref, dq_ref, dk_ref, dv_ref, sq_ref, sk_ref, sv_ref, *refs, n_groups, scale, n_cast):
    cast_in, (k_ref, qbd_ref, vt_ref), cast_out = refs[:n_cast], refs[n_cast:n_cast + 3], refs[n_cast + 3:n_cast * 2 + 3]
    wk_ref, wqvt_ref = refs[n_cast * 2 + 3:]
    T = ATT_TILE
    tiles = x_ref.shape[1] // T

    @pl.when(jnp.logical_and(pl.program_id(0) == 0, pl.program_id(1) == 0))
    def _():
        width = dk_ref.shape[1]
        for n, ref in enumerate((dk_ref, sk_ref)):
            wk_ref[:, n * width:(n + 1) * width] = ref[...].astype(BF16)
        for n, ref in enumerate((dq_ref, sq_ref, dv_ref, sv_ref)):
            wqvt_ref[n * width:(n + 1) * width, :] = ref[...].T.astype(BF16)

    h = _rms(x_ref[0], g_ref[...]).astype(BF16)
    kk = _dot(h, wk_ref[...]).astype(BF16)
    for g in range(n_groups):
        k_ref[0, g] = kk[:, g * LANES:(g + 1) * LANES]
    qv = lax.dot_general(wqvt_ref[...], h, (((1,), (1,)), ((), ())), preferred_element_type=F32)
    nq = n_groups * LANES
    first = lax.broadcasted_iota(jnp.int32, (LANES, T), 0) < HEAD_DIM
    zero = jnp.zeros((LANES, T), BF16)
    for t in range(tiles):
        for g in range(n_groups):
            qg = (qv[g * LANES:(g + 1) * LANES, t * T:(t + 1) * T] * scale).astype(BF16)
            qbd_ref[0, t, g] = jnp.concatenate([jnp.where(first, qg, zero), jnp.where(first, zero, qg)], axis=1)
            vt_ref[0, t, g] = qv[nq + g * LANES:nq + (g + 1) * LANES, t * T:(t + 1) * T].astype(BF16)
    for src, dst in zip(cast_in, cast_out):
        dst[...] = src[...].astype(BF16)


def _proj(x, g, w_in, qk_w, tiles_per_step, layer, cast_weights):
    B, S, D = x.shape
    T = ATT_TILE
    nt = S // T
    tps = tiles_per_step
    steps_per_batch = nt // tps
    n_steps = B * steps_per_batch
    n_groups = 2 * qk_w // LANES
    w_block = lambda col: pl.BlockSpec((None, D, qk_w), lambda b, i: (layer, 0, col), pipeline_mode=pl.Buffered(1))
    cast_in_specs, cast_out_specs, cast_out_shapes = [], [], []
    for w, width, col, hold in cast_weights:
        n_rows = w.shape[1]
        rows = n_rows * hold // n_steps
        assert rows * n_steps == n_rows * hold and rows % BF16_ROWS == 0 and w.shape[2] % width == 0
        cast_in_specs.append(pl.BlockSpec(
            (None, rows, width), lambda b, i, col=col, hold=hold: (layer, (b * steps_per_batch + i) // hold, col)))
        cast_out_specs.append(pl.BlockSpec(
            (rows, width), lambda b, i, hold=hold: ((b * steps_per_batch + i) // hold, 0)))
        cast_out_shapes.append(jax.ShapeDtypeStruct((n_rows, width), BF16))
    kern = functools.partial(_proj_kernel, n_groups=n_groups, scale=HEAD_DIM ** -0.5 * LOG2E,
                             n_cast=len(cast_weights))
    outs = pl.pallas_call(
        kern,
        grid=(B, steps_per_batch),
        in_specs=[
            pl.BlockSpec((1, tps * T, D), lambda b, i: (b, i, 0)),
            pl.BlockSpec((1, D), lambda b, i: (0, 0)),
        ] + [w_block(col) for col in range(6)] + cast_in_specs,
        out_specs=[
            pl.BlockSpec((1, n_groups, tps * T, LANES), lambda b, i: (b, 0, i, 0)),
            pl.BlockSpec((1, tps, n_groups, LANES, 2 * T), lambda b, i: (b, i, 0, 0, 0)),
            pl.BlockSpec((1, tps, n_groups, LANES, T), lambda b, i: (b, i, 0, 0, 0)),
        ] + cast_out_specs,
        out_shape=[
            jax.ShapeDtypeStruct((B, n_groups, S, LANES), BF16),
            jax.ShapeDtypeStruct((B, nt, n_groups, LANES, 2 * T), BF16),
            jax.ShapeDtypeStruct((B, nt, n_groups, LANES, T), BF16),
        ] + cast_out_shapes,
        scratch_shapes=[pltpu.VMEM((D, 2 * qk_w), BF16), pltpu.VMEM((4 * qk_w, D), BF16)],
        compiler_params=pltpu.CompilerParams(
            dimension_semantics=("arbitrary", "arbitrary"), vmem_limit_bytes=VMEM_LIMIT),
        name="proj",
    )(x, g, *([w_in] * 6), *[w for w, _, _, _ in cast_weights])
    return outs[:3], outs[3:]


DIFF_BATCH_BLOCK = 2
SB_BATCH_BLOCK = 1


def _units(k_ref, n_groups):
    return [(bb, g) for bb in range(k_ref.shape[0]) for g in range(n_groups)]


def _key_tile(k_ref, bb, g, j):
    T = ATT_TILE
    start = j * T if isinstance(j, int) else pl.multiple_of(j * T, T)
    return k_ref[bb, g, pl.ds(start, T), :]


def _attention_specs(n_groups, group_block, S, nb):
    T = ATT_TILE
    gb = group_block
    return [
        pl.BlockSpec((nb, 1, n_groups, LANES, 2 * T), lambda b, i: (b, i, gb, 0, 0)),
        pl.BlockSpec((nb, n_groups, S, LANES), lambda b, i: (b, gb, 0, 0)),
        pl.BlockSpec((nb, S // T, n_groups, LANES, T), lambda b, i: (b, 0, gb, 0, 0)),
    ]


def _diff_kernel(lq1_ref, lk1_ref, lq2_ref, lk2_ref, gs_ref, nbd_ref, nbp_ref, qbd_ref, k_ref, vt_ref,
                 y_ref, z_ref, mx_ref, l_ref, ot_ref, *, n_groups, lambda_init):
    T = ATT_TILE
    qi = pl.program_id(1)
    units = list(enumerate(_units(k_ref, n_groups)))

    def score_unit(u, bb, g, j, slot, nb_ref):
        st = _dot(_key_tile(k_ref, bb, g, j), qbd_ref[bb, 0, g])
        if nb_ref is not None:
            nb = nb_ref[g]
            st = jnp.concatenate([st[:, :T] + nb, st[:, T:] + nb], axis=1)
        z_ref[slot, u] = st

    def tile(j, slot, first=False, next_bias=None, prefetch=True):
        ps, alphas = [], []
        for u, _ in units:
            s = z_ref[slot, u]
            m = jnp.max(s, axis=0, keepdims=True)
            if not first:
                m_old = mx_ref[u]
                m = jnp.maximum(m_old, m)
                alphas.append(jnp.exp2(m_old - m))
            p = jnp.exp2(s - m)
            lsum = jnp.sum(p, axis=0, keepdims=True)
            l_ref[u] = lsum if first else alphas[u] * l_ref[u] + lsum
            mx_ref[u] = m
            ps.append(p.astype(BF16))
        for u, (bb, g) in units:
            if prefetch:
                score_unit(u, bb, g, jnp.maximum(j - 1, 0), 1 - slot, next_bias)
            pv = _dot(vt_ref[bb, j, g], ps[u])
            ot_ref[u] = pv if first else alphas[u] * ot_ref[u] + pv

    for u, (bb, g) in units:
        score_unit(u, bb, g, qi, 0, nbd_ref)
    tile(qi, 0, first=True, next_bias=nbp_ref)

    @pl.when(qi > 0)
    def _():
        tile(qi - 1, 1)

    rest = jnp.maximum(qi - 1, 0)

    def pair(t, carry):
        j = qi - 2 - 2 * t
        tile(j, 0)
        tile(j - 1, 1)
        return carry
    lax.fori_loop(0, lax.shift_right_logical(rest, 1), pair, 0)

    @pl.when(jnp.bitwise_and(rest, 1) == 1)
    def _():
        tile(0, 0, prefetch=False)

    lam = (jnp.exp(jnp.sum(lq1_ref[...] * lk1_ref[...], keepdims=True))
           - jnp.exp(jnp.sum(lq2_ref[...] * lk2_ref[...], keepdims=True)) + lambda_init)
    for u, (bb, g) in units:
        ot = ot_ref[u] * (1.0 / l_ref[u])
        o = ot[:, :T] - lam * ot[:, T:]
        o = o * lax.rsqrt(jnp.mean(o * o, axis=0, keepdims=True) + NORM_EPS)
        y = o.T * gs_ref[...] * (1.0 - lambda_init)
        y_ref[bb, :, g * LANES:(g + 1) * LANES] = y.astype(y_ref.dtype)


def _diff_attention(qbd, k, vt, near_diag, near_prev, lq1, lk1, lq2, lk2, gs, lambda_init, n_groups, group_block):
    B, _, S, _ = k.shape
    T = ATT_TILE
    nt = S // T
    nb = DIFF_BATCH_BLOCK
    n_units = nb * n_groups
    kern = functools.partial(_diff_kernel, n_groups=n_groups, lambda_init=lambda_init)
    small = lambda a: pl.BlockSpec(a.shape, lambda b, i: (0,) * a.ndim)
    return pl.pallas_call(
        kern,
        grid=(B // nb, nt),
        in_specs=[small(lq1), small(lk1), small(lq2), small(lk2), small(gs), small(near_diag), small(near_prev)]
        + _attention_specs(n_groups, group_block, S, nb),
        out_specs=pl.BlockSpec((nb, T, n_groups * LANES), lambda b, i: (b, i, 0)),
        out_shape=jax.ShapeDtypeStruct((B, S, n_groups * LANES), BF16),
        scratch_shapes=[
            pltpu.VMEM((2, n_units, T, 2 * T), F32),
            pltpu.VMEM((n_units, 1, 2 * T), F32),
            pltpu.VMEM((n_units, 1, 2 * T), F32),
            pltpu.VMEM((n_units, LANES, 2 * T), F32),
        ],
        compiler_params=pltpu.CompilerParams(
            dimension_semantics=("arbitrary", "arbitrary"), vmem_limit_bytes=VMEM_LIMIT),
        name="diff_attn",
    )(lq1, lk1, lq2, lk2, gs, near_diag, near_prev, qbd, k, vt)


def _sb_kernel(nu_ref, qbd_ref, k_ref, vt_ref, y_ref, z_ref, c_ref, ot_ref, *, n_groups):
    T = ATT_TILE
    qi = pl.program_id(1)
    units = list(enumerate(_units(k_ref, n_groups)))
    kk = lax.broadcasted_iota(jnp.int32, (T, 2 * T), 0)
    qq = lax.broadcasted_iota(jnp.int32, (T, 2 * T), 1)
    causal = kk < jnp.where(qq >= T, qq - T, qq)
    sign = jnp.uint32(0x80000000)

    def score_unit(u, bb, g, j, slot):
        z_ref[slot, u] = _dot(_key_tile(k_ref, bb, g, j), qbd_ref[bb, 0, g])

    def score_tile(j, slot):
        for u, (bb, g) in units:
            score_unit(u, bb, g, j, slot)

    def softplus_tile(slot, u, masked):
        z = z_ref[slot, u]
        neg_abs = lax.bitcast_convert_type(lax.bitcast_convert_type(z, jnp.uint32) | sign, F32)
        sp = jnp.maximum(z, 0.0) + jnp.log2(1.0 + jnp.exp2(neg_abs))
        if masked:
            sp = jnp.where(causal, sp, 0.0)
        return sp.astype(BF16)

    def weight_tile(slot, u, c, incl, masked):
        w = jnp.exp2((z_ref[slot, u] + c) + incl if c is not None else z_ref[slot, u] + incl)
        if masked:
            w = jnp.where(causal, w, 0.0)
        return w.astype(BF16)

    def tile(j, slot, masked, prefetch, first=False):
        sps = [softplus_tile(slot, u, masked) for u, _ in units]
        incls = []
        for u, (bb, g) in units:
            if prefetch:
                score_unit(u, bb, g, jnp.maximum(j - 1, 0), 1 - slot)
            incls.append(_dot(nu_ref[...], sps[u]))
        ws = []
        for u, _ in units:
            ws.append(weight_tile(slot, u, None if first else c_ref[u], incls[u], masked))
            c_ref[u] = incls[u][0:1] if first else c_ref[u] + incls[u][0:1]
        for u, (bb, g) in units:
            pv = _dot(vt_ref[bb, j, g], ws[u])
            ot_ref[u] = pv if first else ot_ref[u] + pv

    def diagonal_pair():
        score_tile(qi, 0)
        sp_d = [softplus_tile(0, u, True) for u, _ in units]
        incl_d = []
        for u, (bb, g) in units:
            score_unit(u, bb, g, qi - 1, 1)
            incl_d.append(_dot(nu_ref[...], sp_d[u]))
        sp_p = [softplus_tile(1, u, False) for u, _ in units]
        incl_p = [_dot(nu_ref[...], sp_p[u]) for u, _ in units]
        for u, (bb, g) in units:
            c_d = incl_d[u][0:1]
            w = jnp.concatenate([weight_tile(1, u, c_d, incl_p[u], False),
                                 weight_tile(0, u, None, incl_d[u], True)], axis=0)
            v = jnp.concatenate([vt_ref[bb, qi - 1, g], vt_ref[bb, qi, g]], axis=1)
            ot_ref[u] = _dot(v, w)
            c_ref[u] = c_d + incl_p[u][0:1]

    @pl.when(qi == 0)
    def _():
        score_tile(0, 0)
        tile(0, 0, True, prefetch=False, first=True)

    @pl.when(qi > 0)
    def _():
        diagonal_pair()

    def alive():
        return jnp.max(c_ref[...]) >= DEAD_LOG2

    rest = jnp.maximum(qi - 1, 0)
    n_pairs = lax.shift_right_logical(rest, 1)

    def pair(carry):
        t, _ = carry
        j = qi - 2 - 2 * t
        score_tile(j, 0)
        tile(j, 0, False, prefetch=True)
        tile(j - 1, 1, False, prefetch=False)
        return t + 1, alive()
    _, go = lax.while_loop(lambda carry: jnp.logical_and(carry[0] < n_pairs, carry[1]), pair,
                           (jnp.int32(0), alive()))

    @pl.when(jnp.logical_and(jnp.bitwise_and(rest, 1) == 1, go))
    def _():
        score_tile(0, 0)
        tile(0, 0, False, prefetch=False)

    for u, (bb, g) in units:
        ot = ot_ref[u]
        o = jnp.concatenate([ot[:HEAD_DIM, :T], ot[HEAD_DIM:, T:]], axis=0)
        y_ref[bb, :, g * LANES:(g + 1) * LANES] = o.T.astype(y_ref.dtype)


def _sb_attention(qbd, k, vt, nu, n_groups, group_block):
    B, _, S, _ = k.shape
    T = ATT_TILE
    nb = SB_BATCH_BLOCK
    n_units = nb * n_groups
    kern = functools.partial(_sb_kernel, n_groups=n_groups)
    return pl.pallas_call(
        kern,
        grid=(B // nb, S // T),
        in_specs=[pl.BlockSpec(nu.shape, lambda b, i: (0, 0))] + _attention_specs(n_groups, group_block, S, nb),
        out_specs=pl.BlockSpec((nb, T, n_groups * LANES), lambda b, i: (b, i, 0)),
        out_shape=jax.ShapeDtypeStruct((B, S, n_groups * LANES), BF16),
        scratch_shapes=[
            pltpu.VMEM((2, n_units, T, 2 * T), F32),
            pltpu.VMEM((n_units, 1, 2 * T), F32),
            pltpu.VMEM((n_units, LANES, 2 * T), F32),
        ],
        compiler_params=pltpu.CompilerParams(
            dimension_semantics=("arbitrary", "arbitrary"), vmem_limit_bytes=VMEM_LIMIT),
        name="sb_attn",
    )(nu, qbd, k, vt)


def _mix_kernel(x_ref, yd_ref, ys_ref, g_ref, wgd_ref, wgs_ref, wbd_ref, wbs_ref, wo_ref, o_ref):
    x = x_ref[...]
    h = _rms(x, g_ref[...]).astype(BF16)
    merged = (jax.nn.sigmoid(_dot(h, wgd_ref[...])) * _dot(yd_ref[...], wbd_ref[...])
              + jax.nn.sigmoid(_dot(h, wgs_ref[...])) * _dot(ys_ref[...], wbs_ref[...]))
    o_ref[...] = x + _dot(merged.astype(BF16), wo_ref[...])


def _mix(x2d, yd, ys, g, wgd, wgs, wbd, wbs, wo, tm):
    N, D = x2d.shape
    const = lambda a: pl.BlockSpec(a.shape, lambda i: (0, 0))
    row = lambda w: pl.BlockSpec((tm, w), lambda i: (i, 0))
    return pl.pallas_call(
        _mix_kernel,
        grid=(N // tm,),
        in_specs=[row(D), row(yd.shape[1]), row(ys.shape[1]),
                  const(g), const(wgd), const(wgs), const(wbd), const(wbs), const(wo)],
        out_specs=row(D),
        out_shape=jax.ShapeDtypeStruct((N, D), F32),
        compiler_params=pltpu.CompilerParams(dimension_semantics=("arbitrary",), vmem_limit_bytes=VMEM_LIMIT),
        name="mix",
    )(x2d, yd, ys, g, wgd, wgs, wbd, wbs, wo)


def _ffn_kernel(x_ref, g_ref, wup_ref, cw_ref, cb_ref, wdn_ref, gf_ref, o_ref, abuf_ref, *, final_norm):
    tm = x_ref.shape[1]
    F = cw_ref.shape[1]
    H = SUBLANES

    @pl.when(pl.program_id(1) == 0)
    def _():
        abuf_ref[0:H] = jnp.zeros((H, F), F32)

    x = x_ref[0]
    h = _rms(x, g_ref[...]).astype(BF16)
    up = _dot(h, wup_ref[...])
    a = up[:, :F]
    abuf_ref[H:H + tm] = a
    conv = (cw_ref[0:1] * abuf_ref[H - 2:H - 2 + tm] + cw_ref[1:2] * abuf_ref[H - 1:H - 1 + tm]
            + cw_ref[2:3] * a + cb_ref[...])
    abuf_ref[0:H] = abuf_ref[tm:tm + H]
    gelu = 0.5 * conv * (1.0 + lax.erf(conv * (2.0 ** -0.5)))
    act = gelu * up[:, F:]
    x = x + _dot(act.astype(BF16), wdn_ref[...])
    o_ref[0] = _rms(x, gf_ref[...]) if final_norm else x


def _ffn(x, g, wup, cw, cb, wdn, gf, tm, final_norm):
    B, S, D = x.shape
    F = cw.shape[1]
    const = lambda a: pl.BlockSpec(a.shape, lambda b, i: (0, 0), pipeline_mode=pl.Buffered(1))
    return pl.pallas_call(
        functools.partial(_ffn_kernel, final_norm=final_norm),
        grid=(B, S // tm),
        in_specs=[pl.BlockSpec((1, tm, D), lambda b, i: (b, i, 0)),
                  const(g), const(wup), const(cw), const(cb), const(wdn), const(gf)],
        out_specs=pl.BlockSpec((1, tm, D), lambda b, i: (b, i, 0)),
        out_shape=jax.ShapeDtypeStruct((B, S, D), F32),
        scratch_shapes=[pltpu.VMEM((tm + SUBLANES, F), F32)],
        compiler_params=pltpu.CompilerParams(
            dimension_semantics=("arbitrary", "arbitrary"), vmem_limit_bytes=VMEM_LIMIT),
        name="ffn",
    )(x, g, wup, cw, cb, wdn, gf)


def _t5_bucket(rel):
    half = N_BUCKETS // 2
    ret = jnp.where(rel > 0, half, 0)
    n = jnp.abs(rel)
    max_exact = half // 2
    nf = jnp.maximum(n, 1).astype(jnp.float32)
    large = max_exact + (jnp.log(nf / max_exact) / math.log(MAX_DISTANCE / max_exact)
                         * (half - max_exact)).astype(jnp.int32)
    large = jnp.minimum(large, half - 1)
    return ret + jnp.where(n < max_exact, n, large)


def _near_bias_tables(rel_bias):
    T = ATT_TILE
    L = 2 * T
    H = rel_bias.shape[1]
    far = rel_bias[_t5_bucket(jnp.asarray(-MAX_DISTANCE))].astype(F32)
    m = jnp.arange(L)
    rel = jnp.where(m < T, -m, L - m)

    def toeplitz(rels):
        v = ((rel_bias[_t5_bucket(rels)].astype(F32) - far) * LOG2E).T
        rows = jnp.tile(v, (1, T))[:, :T * (L - 1)].reshape(H, T, L - 1)
        return rows[:, :, :T]

    kpos = jnp.arange(T)[:, None]
    qpos = jnp.arange(T)[None, :]
    visible = (kpos // CHUNK) <= (qpos // CHUNK)
    return jnp.where(visible[None], toeplitz(rel), -jnp.inf), toeplitz(rel - T)


def kernel(x, norm_mix_g, w_in, diff_lambda_q1, diff_lambda_k1, diff_lambda_q2, diff_lambda_k2, diff_subln_g,
           rel_bias, w_branch_diff, w_branch_sb, w_out, norm_ffn_g, w_ffn_up, ffn_conv_w, ffn_conv_b, w_ffn_down,
           norm_final_g):
    B, S, D = x.shape
    depth = w_in.shape[0]
    T = ATT_TILE
    n_heads_diff = rel_bias.shape[1]
    qk_w = n_heads_diff * 2 * HEAD_DIM
    ngd = qk_w // LANES
    assert S % T == 0 and T % CHUNK == 0 and T >= MAX_DISTANCE and B % DIFF_BATCH_BLOCK == 0
    assert (6 * qk_w) % D == 0
    d_ff = ffn_conv_w.shape[-1]

    near_diag, near_prev = _near_bias_tables(rel_bias)
    neg_upper = -(jnp.arange(T)[None, :] >= jnp.arange(T)[:, None]).astype(BF16)

    for layer in range(depth):
        lambda_init = 0.8 - 0.6 * math.exp(-0.3 * layer)
        gate_col = 6 * qk_w // D
        cast_weights = [(w_in, D, gate_col, 1), (w_in, D, gate_col + 1, 1), (w_branch_diff, D, 0, 1),
                        (w_branch_sb, D, 0, 1), (w_out, D, 0, 1), (w_ffn_up, 2 * d_ff, 0, 1),
                        (w_ffn_down, D, 0, 2)]
        (k, qbd, vt), (wgd, wgs, wbd, wbs, wo, wup, wdn) = _proj(
            x, norm_mix_g[layer][None], w_in, qk_w, tiles_per_step=2, layer=layer, cast_weights=cast_weights)

        row = lambda a: a[layer][None].astype(F32)
        y_diff = _diff_attention(qbd, k, vt, near_diag, near_prev, row(diff_lambda_q1), row(diff_lambda_k1),
                                 row(diff_lambda_q2), row(diff_lambda_k2), row(diff_subln_g), lambda_init, ngd, 0)
        y_sb = _sb_attention(qbd, k, vt, neg_upper, ngd, 1)

        x = _mix(x.reshape(B * S, D), y_diff.reshape(B * S, -1), y_sb.reshape(B * S, -1), norm_mix_g[layer][None],
                 wgd, wgs, wbd, wbs, wo, tm=512).reshape(B, S, D)
        x = _ffn(x, norm_ffn_g[layer][None], wup, ffn_conv_w[layer], ffn_conv_b[layer][None], wdn,
                 norm_final_g[None], tm=512, final_norm=layer == depth - 1)
    return x
```

```python
import functools
import math

import jax
import jax.numpy as jnp
from jax import lax
from jax.experimental import pallas as pl
from jax.experimental.pallas import tpu as pltpu

F32 = jnp.float32
BF16 = jnp.bfloat16

HEAD_DIM = 64
CHUNK = 64
N_BUCKETS = 32
MAX_DISTANCE = 128
NORM_EPS = 1e-6
LOG2E = math.log2(math.e)
DEAD_LOG2 = -150.0

LANES = 128
SUBLANES = 8
BF16_ROWS = 16
ATT_TILE = 256
VMEM_LIMIT = 56 * 1024 * 1024


def _dot(a, b):
    return jnp.dot(a, b, preferred_element_type=F32)


def _rms(x, g):
    return x * lax.rsqrt(jnp.mean(x * x, axis=-1, keepdims=True) + NORM_EPS) * g


def _proj_kernel(x_ref, g_ref, dq_ref, dk_ref, dv_ref, sq_ref, sk_ref, sv_ref, *refs, n_groups, scale, n_cast):
    cast_in, (k_ref, qbd_ref, vt_ref), cast_out = refs[:n_cast], refs[n_cast:n_cast + 3], refs[n_cast + 3:n_cast * 2 + 3]
    wk_ref, wqvt_ref = refs[n_cast * 2 + 3:]
    T = ATT_TILE
    tiles = x_ref.shape[1] // T

    @pl.when(jnp.logical_and(pl.program_id(0) == 0, pl.program_id(1) == 0))
    def _():
        width = dk_ref.shape[1]
        for n, ref in enumerate((dk_ref, sk_ref)):
            wk_ref[:, n * width:(n + 1) * width] = ref[...].astype(BF16)
        for n, ref in enumerate((dq_ref, sq_ref, dv_ref, sv_ref)):
            wqvt_ref[n * width:(n + 1) * width, :] = ref[...].T.astype(BF16)

    h = _rms(x_ref[0], g_ref[...]).astype(BF16)
    kk = _dot(h, wk_ref[...]).astype(BF16)
    for g in range(n_groups):
        k_ref[0, g] = kk[:, g * LANES:(g + 1) * LANES]
    qv = lax.dot_general(wqvt_ref[...], h, (((1,), (1,)), ((), ())), preferred_element_type=F32)
    nq = n_groups * LANES
    first = lax.broadcasted_iota(jnp.int32, (LANES, T), 0) < HEAD_DIM
    zero = jnp.zeros((LANES, T), BF16)
    for t in range(tiles):
        for g in range(n_groups):
            qg = (qv[g * LANES:(g + 1) * LANES, t * T:(t + 1) * T] * scale).astype(BF16)
            qbd_ref[0, t, g] = jnp.concatenate([jnp.where(first, qg, zero), jnp.where(first, zero, qg)], axis=1)
            vt_ref[0, t, g] = qv[nq + g * LANES:nq + (g + 1) * LANES, t * T:(t + 1) * T].astype(BF16)
    for src, dst in zip(cast_in, cast_out):
        dst[...] = src[...].astype(BF16)


def _proj(x, g, w_in, qk_w, tiles_per_step, layer, cast_weights):
    B, S, D = x.shape
    T = ATT_TILE
    nt = S // T
    tps = tiles_per_step
    steps_per_batch = nt // tps
    n_steps = B * steps_per_batch
    n_groups = 2 * qk_w // LANES
    w_block = lambda col: pl.BlockSpec((None, D, qk_w), lambda b, i: (layer, 0, col), pipeline_mode=pl.Buffered(1))
    cast_in_specs, cast_out_specs, cast_out_shapes = [], [], []
    for w, width, col, hold in cast_weights:
        n_rows = w.shape[1]
        rows = n_rows * hold // n_steps
        assert rows * n_steps == n_rows * hold and rows % BF16_ROWS == 0 and w.shape[2] % width == 0
        cast_in_specs.append(pl.BlockSpec(
            (None, rows, width), lambda b, i, col=col, hold=hold: (layer, (b * steps_per_batch + i) // hold, col)))
        cast_out_specs.append(pl.BlockSpec(
            (rows, width), lambda b, i, hold=hold: ((b * steps_per_batch + i) // hold, 0)))
        cast_out_shapes.append(jax.ShapeDtypeStruct((n_rows, width), BF16))
    kern = functools.partial(_proj_kernel, n_groups=n_groups, scale=HEAD_DIM ** -0.5 * LOG2E,
                             n_cast=len(cast_weights))
    outs = pl.pallas_call(
        kern,
        grid=(B, steps_per_batch),
        in_specs=[
            pl.BlockSpec((1, tps * T, D), lambda b, i: (b, i, 0)),
            pl.BlockSpec((1, D), lambda b, i: (0, 0)),
        ] + [w_block(col) for col in range(6)] + cast_in_specs,
        out_specs=[
            pl.BlockSpec((1, n_groups, tps * T, LANES), lambda b, i: (b, 0, i, 0)),
            pl.BlockSpec((1, tps, n_groups, LANES, 2 * T), lambda b, i: (b, i, 0, 0, 0)),
            pl.BlockSpec((1, tps, n_groups, LANES, T), lambda b, i: (b, i, 0, 0, 0)),
        ] + cast_out_specs,
        out_shape=[
            jax.ShapeDtypeStruct((B, n_groups, S, LANES), BF16),
            jax.ShapeDtypeStruct((B, nt, n_groups, LANES, 2 * T), BF16),
            jax.ShapeDtypeStruct((B, nt, n_groups, LANES, T), BF16),
        ] + cast_out_shapes,
        scratch_shapes=[pltpu.VMEM((D, 2 * qk_w), BF16), pltpu.VMEM((4 * qk_w, D), BF16)],
        compiler_params=pltpu.CompilerParams(
            dimension_semantics=("arbitrary", "arbitrary"), vmem_limit_bytes=VMEM_LIMIT),
        name="proj",
    )(x, g, *([w_in] * 6), *[w for w, _, _, _ in cast_weights])
    return outs[:3], outs[3:]


DIFF_BATCH_BLOCK = 2
SB_BATCH_BLOCK = 1


def _units(k_ref, n_groups):
    return [(bb, g) for bb in range(k_ref.shape[0]) for g in range(n_groups)]


def _key_tile(k_ref, bb, g, j):
    T = ATT_TILE
    start = j * T if isinstance(j, int) else pl.multiple_of(j * T, T)
    return k_ref[bb, g, pl.ds(start, T), :]


def _attention_specs(n_groups, group_block, S, nb):
    T = ATT_TILE
    gb = group_block
    return [
        pl.BlockSpec((nb, 1, n_groups, LANES, 2 * T), lambda b, i: (b, i, gb, 0, 0)),
        pl.BlockSpec((nb, n_groups, S, LANES), lambda b, i: (b, gb, 0, 0)),
        pl.BlockSpec((nb, S // T, n_groups, LANES, T), lambda b, i: (b, 0, gb, 0, 0)),
    ]


def _diff_kernel(lq1_ref, lk1_ref, lq2_ref, lk2_ref, gs_ref, nbd_ref, nbp_ref, qbd_ref, k_ref, vt_ref,
                 y_ref, z_ref, mx_ref, l_ref, ot_ref, *, n_groups, lambda_init):
    T = ATT_TILE
    qi = pl.program_id(1)
    units = list(enumerate(_units(k_ref, n_groups)))

    def score_unit(u, bb, g, j, slot, nb_ref):
        st = _dot(_key_tile(k_ref, bb, g, j), qbd_ref[bb, 0, g])
        if nb_ref is not None:
            nb = nb_ref[g]
            st = jnp.concatenate([st[:, :T] + nb, st[:, T:] + nb], axis=1)
        z_ref[slot, u] = st

    def tile(j, slot, next_slot=None, first=False):
        ps, alphas = [], []
        for u, _ in units:
            s = z_ref[slot, u]
            m = jnp.max(s, axis=0, keepdims=True)
            if not first:
                m_old = mx_ref[u]
                m = jnp.maximum(m_old, m)
                alphas.append(jnp.exp2(m_old - m))
            p = jnp.exp2(s - m)
            lsum = jnp.sum(p, axis=0, keepdims=True)
            l_ref[u] = lsum if first else alphas[u] * l_ref[u] + lsum
            mx_ref[u] = m
            ps.append(p.astype(BF16))
        for u, (bb, g) in units:
            if next_slot is not None:
                score_unit(u, bb, g, jnp.maximum(j - 1, 0), next_slot, None)
            pv = _dot(vt_ref[bb, j, g], ps[u])
            ot_ref[u] = pv if first else alphas[u] * ot_ref[u] + pv

    def diagonal_pair():
        for u, (bb, g) in units:
            score_unit(u, bb, g, qi, 0, nbd_ref)
            score_unit(u, bb, g, qi - 1, 1, nbp_ref)
        ps = []
        for u, _ in units:
            s_d, s_p = z_ref[0, u], z_ref[1, u]
            m = jnp.maximum(jnp.max(s_d, axis=0, keepdims=True), jnp.max(s_p, axis=0, keepdims=True))
            p_d, p_p = jnp.exp2(s_d - m), jnp.exp2(s_p - m)
            l_ref[u] = jnp.sum(p_d, axis=0, keepdims=True) + jnp.sum(p_p, axis=0, keepdims=True)
            mx_ref[u] = m
            ps.append(jnp.concatenate([p_p.astype(BF16), p_d.astype(BF16)], axis=0))
        for u, (bb, g) in units:
            score_unit(u, bb, g, jnp.maximum(qi - 2, 0), 2, None)
            v = jnp.concatenate([vt_ref[bb, qi - 1, g], vt_ref[bb, qi, g]], axis=1)
            ot_ref[u] = _dot(v, ps[u])

    @pl.when(qi == 0)
    def _():
        for u, (bb, g) in units:
            score_unit(u, bb, g, 0, 0, nbd_ref)
        tile(0, 0, first=True)

    @pl.when(qi > 0)
    def _():
        diagonal_pair()

    rest = jnp.maximum(qi - 1, 0)

    def pair(t, carry):
        j = qi - 2 - 2 * t
        tile(j, 2, next_slot=0)
        tile(j - 1, 0, next_slot=2)
        return carry
    lax.fori_loop(0, lax.shift_right_logical(rest, 1), pair, 0)

    @pl.when(jnp.bitwise_and(rest, 1) == 1)
    def _():
        tile(0, 2)

    lam = (jnp.exp(jnp.sum(lq1_ref[...] * lk1_ref[...], keepdims=True))
           - jnp.exp(jnp.sum(lq2_ref[...] * lk2_ref[...], keepdims=True)) + lambda_init)
    for u, (bb, g) in units:
        ot = ot_ref[u] * (1.0 / l_ref[u])
        o = ot[:, :T] - lam * ot[:, T:]
        o = o * lax.rsqrt(jnp.mean(o * o, axis=0, keepdims=True) + NORM_EPS)
        y = o.T * gs_ref[...] * (1.0 - lambda_init)
        y_ref[bb, :, g * LANES:(g + 1) * LANES] = y.astype(y_ref.dtype)


def _diff_attention(qbd, k, vt, near_diag, near_prev, lq1, lk1, lq2, lk2, gs, lambda_init, n_groups, group_block):
    B, _, S, _ = k.shape
    T = ATT_TILE
    nt = S // T
    nb = DIFF_BATCH_BLOCK
    n_units = nb * n_groups
    kern = functools.partial(_diff_kernel, n_groups=n_groups, lambda_init=lambda_init)
    small = lambda a: pl.BlockSpec(a.shape, lambda b, i: (0,) * a.ndim)
    return pl.pallas_call(
        kern,
        grid=(B // nb, nt),
        in_specs=[small(lq1), small(lk1), small(lq2), small(lk2), small(gs), small(near_diag), small(near_prev)]
        + _attention_specs(n_groups, group_block, S, nb),
        out_specs=pl.BlockSpec((nb, T, n_groups * LANES), lambda b, i: (b, i, 0)),
        out_shape=jax.ShapeDtypeStruct((B, S, n_groups * LANES), BF16),
        scratch_shapes=[
            pltpu.VMEM((3, n_units, T, 2 * T), F32),
            pltpu.VMEM((n_units, 1, 2 * T), F32),
            pltpu.VMEM((n_units, 1, 2 * T), F32),
            pltpu.VMEM((n_units, LANES, 2 * T), F32),
        ],
        compiler_params=pltpu.CompilerParams(
            dimension_semantics=("arbitrary", "arbitrary"), vmem_limit_bytes=VMEM_LIMIT),
        name="diff_attn",
    )(lq1, lk1, lq2, lk2, gs, near_diag, near_prev, qbd, k, vt)


def _sb_kernel(nu_ref, qbd_ref, k_ref, vt_ref, y_ref, z_ref, c_ref, ot_ref, *, n_groups):
    T = ATT_TILE
    qi = pl.program_id(1)
    units = list(enumerate(_units(k_ref, n_groups)))
    kk = lax.broadcasted_iota(jnp.int32, (T, 2 * T), 0)
    qq = lax.broadcasted_iota(jnp.int32, (T, 2 * T), 1)
    causal = kk < jnp.where(qq >= T, qq - T, qq)
    sign = jnp.uint32(0x80000000)

    def score_unit(u, bb, g, j, slot):
        z_ref[slot, u] = _dot(_key_tile(k_ref, bb, g, j), qbd_ref[bb, 0, g])

    def score_tile(j, slot):
        for u, (bb, g) in units:
            score_unit(u, bb, g, j, slot)

    def softplus_tile(slot, u, masked):
        z = z_ref[slot, u]
        neg_abs = lax.bitcast_convert_type(lax.bitcast_convert_type(z, jnp.uint32) | sign, F32)
        sp = jnp.maximum(z, 0.0) + jnp.log2(1.0 + jnp.exp2(neg_abs))
        if masked:
            sp = jnp.where(causal, sp, 0.0)
        return sp.astype(BF16)

    def weight_tile(slot, u, c, incl, masked):
        w = jnp.exp2((z_ref[slot, u] + c) + incl if c is not None else z_ref[slot, u] + incl)
        if masked:
            w = jnp.where(causal, w, 0.0)
        return w.astype(BF16)

    def tile(j, slot, masked, prefetch, first=False):
        sps = [softplus_tile(slot, u, masked) for u, _ in units]
        incls = []
        for u, (bb, g) in units:
            if prefetch:
                score_unit(u, bb, g, jnp.maximum(j - 1, 0), 1 - slot)
            incls.append(_dot(nu_ref[...], sps[u]))
        ws = []
        for u, _ in units:
            ws.append(weight_tile(slot, u, None if first else c_ref[u], incls[u], masked))
            c_ref[u] = incls[u][0:1] if first else c_ref[u] + incls[u][0:1]
        for u, (bb, g) in units:
            pv = _dot(vt_ref[bb, j, g], ws[u])
            ot_ref[u] = pv if first else ot_ref[u] + pv

    def diagonal_pair():
        score_tile(qi, 0)
        sp_d = [softplus_tile(0, u, True) for u, _ in units]
        incl_d = []
        for u, (bb, g) in units:
            score_unit(u, bb, g, qi - 1, 1)
            incl_d.append(_dot(nu_ref[...], sp_d[u]))
        sp_p = [softplus_tile(1, u, False) for u, _ in units]
        incl_p = [_dot(nu_ref[...], sp_p[u]) for u, _ in units]
        for u, (bb, g) in units:
            c_d = incl_d[u][0:1]
            w = jnp.concatenate([weight_tile(1, u, c_d, incl_p[u], False),
                                 weight_tile(0, u, None, incl_d[u], True)], axis=0)
            v = jnp.concatenate([vt_ref[bb, qi - 1, g], vt_ref[bb, qi, g]], axis=1)
            ot_ref[u] = _dot(v, w)
            c_ref[u] = c_d + incl_p[u][0:1]

    @pl.when(qi == 0)
    def _():
        score_tile(0, 0)
        tile(0, 0, True, prefetch=False, first=True)

    @pl.when(qi > 0)
    def _():
        diagonal_pair()

    def alive():
        return jnp.max(c_ref[...]) >= DEAD_LOG2

    rest = jnp.maximum(qi - 1, 0)
    n_pairs = lax.shift_right_logical(rest, 1)

    def pair(carry):
        t, _ = carry
        j = qi - 2 - 2 * t
        score_tile(j, 0)
        tile(j, 0, False, prefetch=True)
        tile(j - 1, 1, False, prefetch=False)
        return t + 1, alive()
    _, go = lax.while_loop(lambda carry: jnp.logical_and(carry[0] < n_pairs, carry[1]), pair,
                           (jnp.int32(0), alive()))

    @pl.when(jnp.logical_and(jnp.bitwise_and(rest, 1) == 1, go))
    def _():
        score_tile(0, 0)
        tile(0, 0, False, prefetch=False)

    for u, (bb, g) in units:
        ot = ot_ref[u]
        o = jnp.concatenate([ot[:HEAD_DIM, :T], ot[HEAD_DIM:, T:]], axis=0)
        y_ref[bb, :, g * LANES:(g + 1) * LANES] = o.T.astype(y_ref.dtype)


def _sb_attention(qbd, k, vt, nu, n_groups, group_block):
    B, _, S, _ = k.shape
    T = ATT_TILE
    nb = SB_BATCH_BLOCK
    n_units = nb * n_groups
    kern = functools.partial(_sb_kernel, n_groups=n_groups)
    return pl.pallas_call(
        kern,
        grid=(B // nb, S // T),
        in_specs=[pl.BlockSpec(nu.shape, lambda b, i: (0, 0))] + _attention_specs(n_groups, group_block, S, nb),
        out_specs=pl.BlockSpec((nb, T, n_groups * LANES), lambda b, i: (b, i, 0)),
        out_shape=jax.ShapeDtypeStruct((B, S, n_groups * LANES), BF16),
        scratch_shapes=[
            pltpu.VMEM((2, n_units, T, 2 * T), F32),
            pltpu.VMEM((n_units, 1, 2 * T), F32),
            pltpu.VMEM((n_units, LANES, 2 * T), F32),
        ],
        compiler_params=pltpu.CompilerParams(
            dimension_semantics=("arbitrary", "arbitrary"), vmem_limit_bytes=VMEM_LIMIT),
        name="sb_attn",
    )(nu, qbd, k, vt)


def _mix_kernel(x_ref, yd_ref, ys_ref, g_ref, wgd_ref, wgs_ref, wbd_ref, wbs_ref, wo_ref, o_ref):
    x = x_ref[...]
    h = _rms(x, g_ref[...]).astype(BF16)
    merged = (jax.nn.sigmoid(_dot(h, wgd_ref[...])) * _dot(yd_ref[...], wbd_ref[...])
              + jax.nn.sigmoid(_dot(h, wgs_ref[...])) * _dot(ys_ref[...], wbs_ref[...]))
    o_ref[...] = x + _dot(merged.astype(BF16), wo_ref[...])


def _mix(x2d, yd, ys, g, wgd, wgs, wbd, wbs, wo, tm):
    N, D = x2d.shape
    const = lambda a: pl.BlockSpec(a.shape, lambda i: (0, 0))
    row = lambda w: pl.BlockSpec((tm, w), lambda i: (i, 0))
    return pl.pallas_call(
        _mix_kernel,
        grid=(N // tm,),
        in_specs=[row(D), row(yd.shape[1]), row(ys.shape[1]),
                  const(g), const(wgd), const(wgs), const(wbd), const(wbs), const(wo)],
        out_specs=row(D),
        out_shape=jax.ShapeDtypeStruct((N, D), F32),
        compiler_params=pltpu.CompilerParams(dimension_semantics=("arbitrary",), vmem_limit_bytes=VMEM_LIMIT),
        name="mix",
    )(x2d, yd, ys, g, wgd, wgs, wbd, wbs, wo)


def _ffn_kernel(x_ref, g_ref, wup_ref, cw_ref, cb_ref, wdn_ref, gf_ref, o_ref, abuf_ref, *, final_norm):
    tm = x_ref.shape[1]
    F = cw_ref.shape[1]
    H = SUBLANES

    @pl.when(pl.program_id(1) == 0)
    def _():
        abuf_ref[0:H] = jnp.zeros((H, F), F32)

    x = x_ref[0]
    h = _rms(x, g_ref[...]).astype(BF16)
    up = _dot(h, wup_ref[...])
    a = up[:, :F]
    abuf_ref[H:H + tm] = a
    conv = (cw_ref[0:1] * abuf_ref[H - 2:H - 2 + tm] + cw_ref[1:2] * abuf_ref[H - 1:H - 1 + tm]
            + cw_ref[2:3] * a + cb_ref[...])
    abuf_ref[0:H] = abuf_ref[tm:tm + H]
    gelu = 0.5 * conv * (1.0 + lax.erf(conv * (2.0 ** -0.5)))
    act = gelu * up[:, F:]
    x = x + _dot(act.astype(BF16), wdn_ref[...])
    o_ref[0] = _rms(x, gf_ref[...]) if final_norm else x


def _ffn(x, g, wup, cw, cb, wdn, gf, tm, final_norm):
    B, S, D = x.shape
    F = cw.shape[1]
    const = lambda a: pl.BlockSpec(a.shape, lambda b, i: (0, 0), pipeline_mode=pl.Buffered(1))
    return pl.pallas_call(
        functools.partial(_ffn_kernel, final_norm=final_norm),
        grid=(B, S // tm),
        in_specs=[pl.BlockSpec((1, tm, D), lambda b, i: (b, i, 0)),
                  const(g), const(wup), const(cw), const(cb), const(wdn), const(gf)],
        out_specs=pl.BlockSpec((1, tm, D), lambda b, i: (b, i, 0)),
        out_shape=jax.ShapeDtypeStruct((B, S, D), F32),
        scratch_shapes=[pltpu.VMEM((tm + SUBLANES, F), F32)],
        compiler_params=pltpu.CompilerParams(
            dimension_semantics=("arbitrary", "arbitrary"), vmem_limit_bytes=VMEM_LIMIT),
        name="ffn",
    )(x, g, wup, cw, cb, wdn, gf)


def _t5_bucket(rel):
    half = N_BUCKETS // 2
    ret = jnp.where(rel > 0, half, 0)
    n = jnp.abs(rel)
    max_exact = half // 2
    nf = jnp.maximum(n, 1).astype(jnp.float32)
    large = max_exact + (jnp.log(nf / max_exact) / math.log(MAX_DISTANCE / max_exact)
                         * (half - max_exact)).astype(jnp.int32)
    large = jnp.minimum(large, half - 1)
    return ret + jnp.where(n < max_exact, n, large)


def _near_bias_tables(rel_bias):
    T = ATT_TILE
    L = 2 * T
    H = rel_bias.shape[1]
    far = rel_bias[_t5_bucket(jnp.asarray(-MAX_DISTANCE))].astype(F32)
    m = jnp.arange(L)
    rel = jnp.where(m < T, -m, L - m)

    def toeplitz(rels):
        v = ((rel_bias[_t5_bucket(rels)].astype(F32) - far) * LOG2E).T
        rows = jnp.tile(v, (1, T))[:, :T * (L - 1)].reshape(H, T, L - 1)
        return rows[:, :, :T]

    kpos = jnp.arange(T)[:, None]
    qpos = jnp.arange(T)[None, :]
    visible = (kpos // CHUNK) <= (qpos // CHUNK)
    return jnp.where(visible[None], toeplitz(rel), -jnp.inf), toeplitz(rel - T)


def kernel(x, norm_mix_g, w_in, diff_lambda_q1, diff_lambda_k1, diff_lambda_q2, diff_lambda_k2, diff_subln_g,
           rel_bias, w_branch_diff, w_branch_sb, w_out, norm_ffn_g, w_ffn_up, ffn_conv_w, ffn_conv_b, w_ffn_down,
           norm_final_g):
    B, S, D = x.shape
    depth = w_in.shape[0]
    T = ATT_TILE
    n_heads_diff = rel_bias.shape[1]
    qk_w = n_heads_diff * 2 * HEAD_DIM
    ngd = qk_w // LANES
    assert S % T == 0 and T % CHUNK == 0 and T >= MAX_DISTANCE and B % DIFF_BATCH_BLOCK == 0
    assert (6 * qk_w) % D == 0
    d_ff = ffn_conv_w.shape[-1]

    near_diag, near_prev = _near_bias_tables(rel_bias)
    neg_upper = -(jnp.arange(T)[None, :] >= jnp.arange(T)[:, None]).astype(BF16)

    for layer in range(depth):
        lambda_init = 0.8 - 0.6 * math.exp(-0.3 * layer)
        gate_col = 6 * qk_w // D
        cast_weights = [(w_in, D, gate_col, 1), (w_in, D, gate_col + 1, 1), (w_branch_diff, D, 0, 1),
                        (w_branch_sb, D, 0, 1), (w_out, D, 0, 1), (w_ffn_up, 2 * d_ff, 0, 1),
                        (w_ffn_down, D, 0, 2)]
        (k, qbd, vt), (wgd, wgs, wbd, wbs, wo, wup, wdn) = _proj(
            x, norm_mix_g[layer][None], w_in, qk_w, tiles_per_step=2, layer=layer, cast_weights=cast_weights)

        row = lambda a: a[layer][None].astype(F32)
        y_diff = _diff_attention(qbd, k, vt, near_diag, near_prev, row(diff_lambda_q1), row(diff_lambda_k1),
                                 row(diff_lambda_q2), row(diff_lambda_k2), row(diff_subln_g), lambda_init, ngd, 0)
        y_sb = _sb_attention(qbd, k, vt, neg_upper, ngd, 1)

        x = _mix(x.reshape(B * S, D), y_diff.reshape(B * S, -1), y_sb.reshape(B * S, -1), norm_mix_g[layer][None],
                 wgd, wgs, wbd, wbs, wo, tm=512).reshape(B, S, D)
        x = _ffn(x, norm_ffn_g[layer][None], wup, ffn_conv_w[layer], ffn_conv_b[layer][None], wdn,
                 norm_final_g[None], tm=512, final_norm=layer == depth - 1)
    return x
```

```python
import functools
import math

import jax
import jax.numpy as jnp
from jax import lax
from jax.experimental import pallas as pl
from jax.experimental.pallas import tpu as pltpu

F32 = jnp.float32
BF16 = jnp.bfloat16

HEAD_DIM = 64
CHUNK = 64
N_BUCKETS = 32
MAX_DISTANCE = 128
NORM_EPS = 1e-6
LOG2E = math.log2(math.e)
DEAD_LOG2 = -150.0

LANES = 128
SUBLANES = 8
BF16_ROWS = 16
ATT_TILE = 256
VMEM_LIMIT = 56 * 1024 * 1024


def _dot(a, b):
    return jnp.dot(a, b, preferred_element_type=F32)


def _rms(x, g):
    return x * lax.rsqrt(jnp.mean(x * x, axis=-1, keepdims=True) + NORM_EPS) * g


def _proj_kernel(x_ref, g_ref, dq_ref, dk_ref, dv_ref, sq_ref, sk_ref, sv_ref, *refs, n_groups, scale, n_cast):
    cast_in, (k_ref, qbd_ref, vt_ref), cast_out = refs[:n_cast], refs[n_cast:n_cast + 3], refs[n_cast + 3:n_cast * 2 + 3]
    wk_ref, wqvt_ref = refs[n_cast * 2 + 3:]
    T = ATT_TILE
    tiles = x_ref.shape[1] // T

    @pl.when(jnp.logical_and(pl.program_id(0) == 0, pl.program_id(1) == 0))
    def _():
        width = dk_ref.shape[1]
        for n, ref in enumerate((dk_ref, sk_ref)):
            wk_ref[:, n * width:(n + 1) * width] = ref[...].astype(BF16)
        for n, ref in enumerate((dq_ref, sq_ref, dv_ref, sv_ref)):
            wqvt_ref[n * width:(n + 1) * width, :] = ref[...].T.astype(BF16)

    h = _rms(x_ref[0], g_ref[...]).astype(BF16)
    kk = _dot(h, wk_ref[...]).astype(BF16)
    for g in range(n_groups):
        k_ref[0, g] = kk[:, g * LANES:(g + 1) * LANES]
    qv = lax.dot_general(wqvt_ref[...], h, (((1,), (1,)), ((), ())), preferred_element_type=F32)
    nq = n_groups * LANES
    first = lax.broadcasted_iota(jnp.int32, (LANES, T), 0) < HEAD_DIM
    zero = jnp.zeros((LANES, T), BF16)
    for t in range(tiles):
        for g in range(n_groups):
            qg = (qv[g * LANES:(g + 1) * LANES, t * T:(t + 1) * T] * scale).astype(BF16)
            qbd_ref[0, t, g] = jnp.concatenate([jnp.where(first, qg, zero), jnp.where(first, zero, qg)], axis=1)
            vt_ref[0, t, g] = qv[nq + g * LANES:nq + (g + 1) * LANES, t * T:(t + 1) * T].astype(BF16)
    for src, dst in zip(cast_in, cast_out):
        dst[...] = src[...].astype(BF16)


def _proj(x, g, w_in, qk_w, tiles_per_step, layer, cast_weights):
    B, S, D = x.shape
    T = ATT_TILE
    nt = S // T
    tps = tiles_per_step
    steps_per_batch = nt // tps
    n_steps = B * steps_per_batch
    n_groups = 2 * qk_w // LANES
    w_block = lambda col: pl.BlockSpec((None, D, qk_w), lambda b, i: (layer, 0, col), pipeline_mode=pl.Buffered(1))
    cast_in_specs, cast_out_specs, cast_out_shapes = [], [], []
    for w, width, col, hold in cast_weights:
        n_rows = w.shape[1]
        rows = n_rows * hold // n_steps
        assert rows * n_steps == n_rows * hold and rows % BF16_ROWS == 0 and w.shape[2] % width == 0
        cast_in_specs.append(pl.BlockSpec(
            (None, rows, width), lambda b, i, col=col, hold=hold: (layer, (b * steps_per_batch + i) // hold, col)))
        cast_out_specs.append(pl.BlockSpec(
            (rows, width), lambda b, i, hold=hold: ((b * steps_per_batch + i) // hold, 0)))
        cast_out_shapes.append(jax.ShapeDtypeStruct((n_rows, width), BF16))
    kern = functools.partial(_proj_kernel, n_groups=n_groups, scale=HEAD_DIM ** -0.5 * LOG2E,
                             n_cast=len(cast_weights))
    outs = pl.pallas_call(
        kern,
        grid=(B, steps_per_batch),
        in_specs=[
            pl.BlockSpec((1, tps * T, D), lambda b, i: (b, i, 0)),
            pl.BlockSpec((1, D), lambda b, i: (0, 0)),
        ] + [w_block(col) for col in range(6)] + cast_in_specs,
        out_specs=[
            pl.BlockSpec((1, n_groups, tps * T, LANES), lambda b, i: (b, 0, i, 0)),
            pl.BlockSpec((1, tps, n_groups, LANES, 2 * T), lambda b, i: (b, i, 0, 0, 0)),
            pl.BlockSpec((1, tps, n_groups, LANES, T), lambda b, i: (b, i, 0, 0, 0)),
        ] + cast_out_specs,
        out_shape=[
            jax.ShapeDtypeStruct((B, n_groups, S, LANES), BF16),
            jax.ShapeDtypeStruct((B, nt, n_groups, LANES, 2 * T), BF16),
            jax.ShapeDtypeStruct((B, nt, n_groups, LANES, T), BF16),
        ] + cast_out_shapes,
        scratch_shapes=[pltpu.VMEM((D, 2 * qk_w), BF16), pltpu.VMEM((4 * qk_w, D), BF16)],
        compiler_params=pltpu.CompilerParams(
            dimension_semantics=("arbitrary", "arbitrary"), vmem_limit_bytes=VMEM_LIMIT),
        name="proj",
    )(x, g, *([w_in] * 6), *[w for w, _, _, _ in cast_weights])
    return outs[:3], outs[3:]


DIFF_BATCH_BLOCK = 2
SB_BATCH_BLOCK = 1


def _units(k_ref, n_groups):
    return [(bb, g) for bb in range(k_ref.shape[0]) for g in range(n_groups)]


def _key_tile(k_ref, bb, g, j):
    T = ATT_TILE
    start = j * T if isinstance(j, int) else pl.multiple_of(j * T, T)
    return k_ref[bb, g, pl.ds(start, T), :]


def _attention_specs(n_groups, group_block, S, nb):
    T = ATT_TILE
    gb = group_block
    return [
        pl.BlockSpec((nb, 1, n_groups, LANES, 2 * T), lambda b, i: (b, i, gb, 0, 0)),
        pl.BlockSpec((nb, n_groups, S, LANES), lambda b, i: (b, gb, 0, 0)),
        pl.BlockSpec((nb, S // T, n_groups, LANES, T), lambda b, i: (b, 0, gb, 0, 0)),
    ]


def _diff_kernel(lq1_ref, lk1_ref, lq2_ref, lk2_ref, gs_ref, nbd_ref, nbp_ref, qbd_ref, k_ref, vt_ref,
                 y_ref, z_ref, mx_ref, l_ref, ot_ref, *, n_groups, lambda_init):
    T = ATT_TILE
    qi = pl.program_id(1)
    units = list(enumerate(_units(k_ref, n_groups)))

    def score_unit(u, bb, g, j, slot, nb_ref):
        st = _dot(_key_tile(k_ref, bb, g, j), qbd_ref[bb, 0, g])
        if nb_ref is not None:
            nb = nb_ref[g]
            st = jnp.concatenate([st[:, :T] + nb, st[:, T:] + nb], axis=1)
        z_ref[slot, u] = st

    def tile(j, slot, next_slot=None, first=False):
        ps, alphas = [], []
        for u, _ in units:
            s = z_ref[slot, u]
            m = jnp.max(s, axis=0, keepdims=True)
            if not first:
                m_old = mx_ref[u]
                m = jnp.maximum(m_old, m)
                alphas.append(jnp.exp2(m_old - m))
            p = jnp.exp2(s - m)
            lsum = jnp.sum(p, axis=0, keepdims=True)
            l_ref[u] = lsum if first else alphas[u] * l_ref[u] + lsum
            mx_ref[u] = m
            ps.append(p.astype(BF16))
        for u, (bb, g) in units:
            if next_slot is not None:
                score_unit(u, bb, g, jnp.maximum(j - 1, 0), next_slot, None)
            pv = _dot(vt_ref[bb, j, g], ps[u])
            ot_ref[u] = pv if first else alphas[u] * ot_ref[u] + pv

    def diagonal_pair(prefetch):
        for u, (bb, g) in units:
            score_unit(u, bb, g, qi, 0, nbd_ref)
            score_unit(u, bb, g, qi - 1, 1, nbp_ref)
        ps = []
        for u, _ in units:
            s_d, s_p = z_ref[0, u], z_ref[1, u]
            m = jnp.maximum(jnp.max(s_d, axis=0, keepdims=True), jnp.max(s_p, axis=0, keepdims=True))
            p_d, p_p = jnp.exp2(s_d - m), jnp.exp2(s_p - m)
            l_ref[u] = jnp.sum(p_d, axis=0, keepdims=True) + jnp.sum(p_p, axis=0, keepdims=True)
            mx_ref[u] = m
            ps.append(jnp.concatenate([p_p.astype(BF16), p_d.astype(BF16)], axis=0))
        for u, (bb, g) in units:
            if prefetch:
                score_unit(u, bb, g, qi - 2, 2, None)
            v = jnp.concatenate([vt_ref[bb, qi - 1, g], vt_ref[bb, qi, g]], axis=1)
            ot_ref[u] = _dot(v, ps[u])

    @pl.when(qi == 0)
    def _():
        for u, (bb, g) in units:
            score_unit(u, bb, g, 0, 0, nbd_ref)
        tile(0, 0, first=True)

    @pl.when(qi == 1)
    def _():
        diagonal_pair(prefetch=False)

    @pl.when(qi > 1)
    def _():
        diagonal_pair(prefetch=True)

    rest = jnp.maximum(qi - 1, 0)
    even = jnp.bitwise_and(rest, 1) == 0
    ends_in_pair = jnp.logical_and(even, rest > 0)

    def pair(t, carry):
        j = qi - 2 - 2 * t
        tile(j, 2, next_slot=0)
        tile(j - 1, 0, next_slot=2)
        return carry
    lax.fori_loop(0, lax.shift_right_logical(rest, 1) - ends_in_pair.astype(jnp.int32), pair, 0)

    @pl.when(ends_in_pair)
    def _():
        tile(1, 2, next_slot=0)
        tile(0, 0)

    @pl.when(jnp.logical_not(even))
    def _():
        tile(0, 2)

    lam = (jnp.exp(jnp.sum(lq1_ref[...] * lk1_ref[...], keepdims=True))
           - jnp.exp(jnp.sum(lq2_ref[...] * lk2_ref[...], keepdims=True)) + lambda_init)
    for u, (bb, g) in units:
        ot = ot_ref[u] * (1.0 / l_ref[u])
        o = ot[:, :T] - lam * ot[:, T:]
        o = o * lax.rsqrt(jnp.mean(o * o, axis=0, keepdims=True) + NORM_EPS)
        y = o.T * gs_ref[...] * (1.0 - lambda_init)
        y_ref[bb, :, g * LANES:(g + 1) * LANES] = y.astype(y_ref.dtype)


def _diff_attention(qbd, k, vt, near_diag, near_prev, lq1, lk1, lq2, lk2, gs, lambda_init, n_groups, group_block):
    B, _, S, _ = k.shape
    T = ATT_TILE
    nt = S // T
    nb = DIFF_BATCH_BLOCK
    n_units = nb * n_groups
    kern = functools.partial(_diff_kernel, n_groups=n_groups, lambda_init=lambda_init)
    small = lambda a: pl.BlockSpec(a.shape, lambda b, i: (0,) * a.ndim)
    return pl.pallas_call(
        kern,
        grid=(B // nb, nt),
        in_specs=[small(lq1), small(lk1), small(lq2), small(lk2), small(gs), small(near_diag), small(near_prev)]
        + _attention_specs(n_groups, group_block, S, nb),
        out_specs=pl.BlockSpec((nb, T, n_groups * LANES), lambda b, i: (b, i, 0)),
        out_shape=jax.ShapeDtypeStruct((B, S, n_groups * LANES), BF16),
        scratch_shapes=[
            pltpu.VMEM((3, n_units, T, 2 * T), F32),
            pltpu.VMEM((n_units, 1, 2 * T), F32),
            pltpu.VMEM((n_units, 1, 2 * T), F32),
            pltpu.VMEM((n_units, LANES, 2 * T), F32),
        ],
        compiler_params=pltpu.CompilerParams(
            dimension_semantics=("arbitrary", "arbitrary"), vmem_limit_bytes=VMEM_LIMIT),
        name="diff_attn",
    )(lq1, lk1, lq2, lk2, gs, near_diag, near_prev, qbd, k, vt)


def _sb_kernel(nu_ref, qbd_ref, k_ref, vt_ref, y_ref, z_ref, c_ref, ot_ref, *, n_groups):
    T = ATT_TILE
    qi = pl.program_id(1)
    units = list(enumerate(_units(k_ref, n_groups)))
    kk = lax.broadcasted_iota(jnp.int32, (T, 2 * T), 0)
    qq = lax.broadcasted_iota(jnp.int32, (T, 2 * T), 1)
    causal = kk < jnp.where(qq >= T, qq - T, qq)
    sign = jnp.uint32(0x80000000)

    def score_unit(u, bb, g, j, slot):
        z_ref[slot, u] = _dot(_key_tile(k_ref, bb, g, j), qbd_ref[bb, 0, g])

    def score_tile(j, slot):
        for u, (bb, g) in units:
            score_unit(u, bb, g, j, slot)

    def softplus_tile(slot, u, masked):
        z = z_ref[slot, u]
        neg_abs = lax.bitcast_convert_type(lax.bitcast_convert_type(z, jnp.uint32) | sign, F32)
        sp = jnp.maximum(z, 0.0) + jnp.log2(1.0 + jnp.exp2(neg_abs))
        if masked:
            sp = jnp.where(causal, sp, 0.0)
        return sp.astype(BF16)

    def weight_tile(slot, u, c, incl, masked):
        w = jnp.exp2((z_ref[slot, u] + c) + incl if c is not None else z_ref[slot, u] + incl)
        if masked:
            w = jnp.where(causal, w, 0.0)
        return w.astype(BF16)

    def tile(j, slot, masked, prefetch, first=False):
        sps = [softplus_tile(slot, u, masked) for u, _ in units]
        incls = []
        for u, (bb, g) in units:
            if prefetch:
                score_unit(u, bb, g, jnp.maximum(j - 1, 0), 1 - slot)
            incls.append(_dot(nu_ref[...], sps[u]))
        ws = []
        for u, _ in units:
            ws.append(weight_tile(slot, u, None if first else c_ref[u], incls[u], masked))
            c_ref[u] = incls[u][0:1] if first else c_ref[u] + incls[u][0:1]
        for u, (bb, g) in units:
            pv = _dot(vt_ref[bb, j, g], ws[u])
            ot_ref[u] = pv if first else ot_ref[u] + pv

    def diagonal_pair():
        score_tile(qi, 0)
        sp_d = [softplus_tile(0, u, True) for u, _ in units]
        incl_d = []
        for u, (bb, g) in units:
            score_unit(u, bb, g, qi - 1, 1)
            incl_d.append(_dot(nu_ref[...], sp_d[u]))
        sp_p = [softplus_tile(1, u, False) for u, _ in units]
        incl_p = [_dot(nu_ref[...], sp_p[u]) for u, _ in units]
        for u, (bb, g) in units:
            c_d = incl_d[u][0:1]
            w = jnp.concatenate([weight_tile(1, u, c_d, incl_p[u], False),
                                 weight_tile(0, u, None, incl_d[u], True)], axis=0)
            v = jnp.concatenate([vt_ref[bb, qi - 1, g], vt_ref[bb, qi, g]], axis=1)
            ot_ref[u] = _dot(v, w)
            c_ref[u] = c_d + incl_p[u][0:1]

    @pl.when(qi == 0)
    def _():
        score_tile(0, 0)
        tile(0, 0, True, prefetch=False, first=True)

    @pl.when(qi > 0)
    def _():
        diagonal_pair()

    def alive():
        return jnp.max(c_ref[...]) >= DEAD_LOG2

    rest = jnp.maximum(qi - 1, 0)
    n_pairs = lax.shift_right_logical(rest, 1)

    def pair(carry):
        t, _ = carry
        j = qi - 2 - 2 * t
        score_tile(j, 0)
        tile(j, 0, False, prefetch=True)
        tile(j - 1, 1, False, prefetch=False)
        return t + 1, alive()
    _, go = lax.while_loop(lambda carry: jnp.logical_and(carry[0] < n_pairs, carry[1]), pair,
                           (jnp.int32(0), alive()))

    @pl.when(jnp.logical_and(jnp.bitwise_and(rest, 1) == 1, go))
    def _():
        score_tile(0, 0)
        tile(0, 0, False, prefetch=False)

    for u, (bb, g) in units:
        ot = ot_ref[u]
        o = jnp.concatenate([ot[:HEAD_DIM, :T], ot[HEAD_DIM:, T:]], axis=0)
        y_ref[bb, :, g * LANES:(g + 1) * LANES] = o.T.astype(y_ref.dtype)


def _sb_attention(qbd, k, vt, nu, n_groups, group_block):
    B, _, S, _ = k.shape
    T = ATT_TILE
    nb = SB_BATCH_BLOCK
    n_units = nb * n_groups
    kern = functools.partial(_sb_kernel, n_groups=n_groups)
    return pl.pallas_call(
        kern,
        grid=(B // nb, S // T),
        in_specs=[pl.BlockSpec(nu.shape, lambda b, i: (0, 0))] + _attention_specs(n_groups, group_block, S, nb),
        out_specs=pl.BlockSpec((nb, T, n_groups * LANES), lambda b, i: (b, i, 0)),
        out_shape=jax.ShapeDtypeStruct((B, S, n_groups * LANES), BF16),
        scratch_shapes=[
            pltpu.VMEM((2, n_units, T, 2 * T), F32),
            pltpu.VMEM((n_units, 1, 2 * T), F32),
            pltpu.VMEM((n_units, LANES, 2 * T), F32),
        ],
        compiler_params=pltpu.CompilerParams(
            dimension_semantics=("arbitrary", "arbitrary"), vmem_limit_bytes=VMEM_LIMIT),
        name="sb_attn",
    )(nu, qbd, k, vt)


def _mix_kernel(x_ref, yd_ref, ys_ref, g_ref, wgd_ref, wgs_ref, wbd_ref, wbs_ref, wo_ref, o_ref):
    x = x_ref[...]
    h = _rms(x, g_ref[...]).astype(BF16)
    merged = (jax.nn.sigmoid(_dot(h, wgd_ref[...])) * _dot(yd_ref[...], wbd_ref[...])
              + jax.nn.sigmoid(_dot(h, wgs_ref[...])) * _dot(ys_ref[...], wbs_ref[...]))
    o_ref[...] = x + _dot(merged.astype(BF16), wo_ref[...])


def _mix(x2d, yd, ys, g, wgd, wgs, wbd, wbs, wo, tm):
    N, D = x2d.shape
    const = lambda a: pl.BlockSpec(a.shape, lambda i: (0, 0))
    row = lambda w: pl.BlockSpec((tm, w), lambda i: (i, 0))
    return pl.pallas_call(
        _mix_kernel,
        grid=(N // tm,),
        in_specs=[row(D), row(yd.shape[1]), row(ys.shape[1]),
                  const(g), const(wgd), const(wgs), const(wbd), const(wbs), const(wo)],
        out_specs=row(D),
        out_shape=jax.ShapeDtypeStruct((N, D), F32),
        compiler_params=pltpu.CompilerParams(dimension_semantics=("arbitrary",), vmem_limit_bytes=VMEM_LIMIT),
        name="mix",
    )(x2d, yd, ys, g, wgd, wgs, wbd, wbs, wo)


def _ffn_kernel(x_ref, g_ref, wup_ref, cw_ref, cb_ref, wdn_ref, gf_ref, o_ref, abuf_ref, *, final_norm):
    tm = x_ref.shape[1]
    F = cw_ref.shape[1]
    H = SUBLANES

    @pl.when(pl.program_id(1) == 0)
    def _():
        abuf_ref[0:H] = jnp.zeros((H, F), F32)

    x = x_ref[0]
    h = _rms(x, g_ref[...]).astype(BF16)
    up = _dot(h, wup_ref[...])
    a = up[:, :F]
    abuf_ref[H:H + tm] = a
    conv = (cw_ref[0:1] * abuf_ref[H - 2:H - 2 + tm] + cw_ref[1:2] * abuf_ref[H - 1:H - 1 + tm]
            + cw_ref[2:3] * a + cb_ref[...])
    abuf_ref[0:H] = abuf_ref[tm:tm + H]
    gelu = 0.5 * conv * (1.0 + lax.erf(conv * (2.0 ** -0.5)))
    act = gelu * up[:, F:]
    x = x + _dot(act.astype(BF16), wdn_ref[...])
    o_ref[0] = _rms(x, gf_ref[...]) if final_norm else x


def _ffn(x, g, wup, cw, cb, wdn, gf, tm, final_norm):
    B, S, D = x.shape
    F = cw.shape[1]
    const = lambda a: pl.BlockSpec(a.shape, lambda b, i: (0, 0), pipeline_mode=pl.Buffered(1))
    return pl.pallas_call(
        functools.partial(_ffn_kernel, final_norm=final_norm),
        grid=(B, S // tm),
        in_specs=[pl.BlockSpec((1, tm, D), lambda b, i: (b, i, 0)),
                  const(g), const(wup), const(cw), const(cb), const(wdn), const(gf)],
        out_specs=pl.BlockSpec((1, tm, D), lambda b, i: (b, i, 0)),
        out_shape=jax.ShapeDtypeStruct((B, S, D), F32),
        scratch_shapes=[pltpu.VMEM((tm + SUBLANES, F), F32)],
        compiler_params=pltpu.CompilerParams(
            dimension_semantics=("arbitrary", "arbitrary"), vmem_limit_bytes=VMEM_LIMIT),
        name="ffn",
    )(x, g, wup, cw, cb, wdn, gf)


def _t5_bucket(rel):
    half = N_BUCKETS // 2
    ret = jnp.where(rel > 0, half, 0)
    n = jnp.abs(rel)
    max_exact = half // 2
    nf = jnp.maximum(n, 1).astype(jnp.float32)
    large = max_exact + (jnp.log(nf / max_exact) / math.log(MAX_DISTANCE / max_exact)
                         * (half - max_exact)).astype(jnp.int32)
    large = jnp.minimum(large, half - 1)
    return ret + jnp.where(n < max_exact, n, large)


def _near_bias_tables(rel_bias):
    T = ATT_TILE
    L = 2 * T
    H = rel_bias.shape[1]
    far = rel_bias[_t5_bucket(jnp.asarray(-MAX_DISTANCE))].astype(F32)
    m = jnp.arange(L)
    rel = jnp.where(m < T, -m, L - m)

    def toeplitz(rels):
        v = ((rel_bias[_t5_bucket(rels)].astype(F32) - far) * LOG2E).T
        rows = jnp.tile(v, (1, T))[:, :T * (L - 1)].reshape(H, T, L - 1)
        return rows[:, :, :T]

    kpos = jnp.arange(T)[:, None]
    qpos = jnp.arange(T)[None, :]
    visible = (kpos // CHUNK) <= (qpos // CHUNK)
    return jnp.where(visible[None], toeplitz(rel), -jnp.inf), toeplitz(rel - T)


def kernel(x, norm_mix_g, w_in, diff_lambda_q1, diff_lambda_k1, diff_lambda_q2, diff_lambda_k2, diff_subln_g,
           rel_bias, w_branch_diff, w_branch_sb, w_out, norm_ffn_g, w_ffn_up, ffn_conv_w, ffn_conv_b, w_ffn_down,
           norm_final_g):
    B, S, D = x.shape
    depth = w_in.shape[0]
    T = ATT_TILE
    n_heads_diff = rel_bias.shape[1]
    qk_w = n_heads_diff * 2 * HEAD_DIM
    ngd = qk_w // LANES
    assert S % T == 0 and T % CHUNK == 0 and T >= MAX_DISTANCE and B % DIFF_BATCH_BLOCK == 0
    assert (6 * qk_w) % D == 0
    d_ff = ffn_conv_w.shape[-1]

    near_diag, near_prev = _near_bias_tables(rel_bias)
    neg_upper = -(jnp.arange(T)[None, :] >= jnp.arange(T)[:, None]).astype(BF16)

    for layer in range(depth):
        lambda_init = 0.8 - 0.6 * math.exp(-0.3 * layer)
        gate_col = 6 * qk_w // D
        cast_weights = [(w_in, D, gate_col, 1), (w_in, D, gate_col + 1, 1), (w_branch_diff, D, 0, 1),
                        (w_branch_sb, D, 0, 1), (w_out, D, 0, 1), (w_ffn_up, 2 * d_ff, 0, 1),
                        (w_ffn_down, D, 0, 2)]
        (k, qbd, vt), (wgd, wgs, wbd, wbs, wo, wup, wdn) = _proj(
            x, norm_mix_g[layer][None], w_in, qk_w, tiles_per_step=2, layer=layer, cast_weights=cast_weights)

        row = lambda a: a[layer][None].astype(F32)
        y_diff = _diff_attention(qbd, k, vt, near_diag, near_prev, row(diff_lambda_q1), row(diff_lambda_k1),
                                 row(diff_lambda_q2), row(diff_lambda_k2), row(diff_subln_g), lambda_init, ngd, 0)
        y_sb = _sb_attention(qbd, k, vt, neg_upper, ngd, 1)

        x = _mix(x.reshape(B * S, D), y_diff.reshape(B * S, -1), y_sb.reshape(B * S, -1), norm_mix_g[layer][None],
                 wgd, wgs, wbd, wbs, wo, tm=512).reshape(B, S, D)
        x = _ffn(x, norm_ffn_g[layer][None], wup, ffn_conv_w[layer], ffn_conv_b[layer][None], wdn,
                 norm_final_g[None], tm=512, final_norm=layer == depth - 1)
    return x
```

```python
import functools
import math

import jax
import jax.numpy as jnp
from jax import lax
from jax.experimental import pallas as pl
from jax.experimental.pallas import tpu as pltpu

F32 = jnp.float32
BF16 = jnp.bfloat16

HEAD_DIM = 64
CHUNK = 64
N_BUCKETS = 32
MAX_DISTANCE = 128
NORM_EPS = 1e-6
LOG2E = math.log2(math.e)
DEAD_LOG2 = -150.0

LANES = 128
SUBLANES = 8
BF16_ROWS = 16
ATT_TILE = 256
VMEM_LIMIT = 56 * 1024 * 1024


def _dot(a, b):
    return jnp.dot(a, b, preferred_element_type=F32)


def _rms(x, g):
    return x * lax.rsqrt(jnp.mean(x * x, axis=-1, keepdims=True) + NORM_EPS) * g


def _proj_kernel(x_ref, g_ref, dq_ref, dk_ref, dv_ref, sq_ref, sk_ref, sv_ref, *refs, n_groups, scale, n_cast):
    cast_in, (k_ref, qbd_ref, vt_ref), cast_out = refs[:n_cast], refs[n_cast:n_cast + 3], refs[n_cast + 3:n_cast * 2 + 3]
    wk_ref, wqvt_ref = refs[n_cast * 2 + 3:]
    T = ATT_TILE
    tiles = x_ref.shape[1] // T

    @pl.when(jnp.logical_and(pl.program_id(0) == 0, pl.program_id(1) == 0))
    def _():
        width = dk_ref.shape[1]
        for n, ref in enumerate((dk_ref, sk_ref)):
            wk_ref[:, n * width:(n + 1) * width] = ref[...].astype(BF16)
        for n, ref in enumerate((dq_ref, sq_ref, dv_ref, sv_ref)):
            wqvt_ref[n * width:(n + 1) * width, :] = ref[...].T.astype(BF16)

    h = _rms(x_ref[0], g_ref[...]).astype(BF16)
    kk = _dot(h, wk_ref[...]).astype(BF16)
    for g in range(n_groups):
        k_ref[0, g] = kk[:, g * LANES:(g + 1) * LANES]
    qv = lax.dot_general(wqvt_ref[...], h, (((1,), (1,)), ((), ())), preferred_element_type=F32)
    nq = n_groups * LANES
    first = lax.broadcasted_iota(jnp.int32, (LANES, T), 0) < HEAD_DIM
    zero = jnp.zeros((LANES, T), BF16)
    for t in range(tiles):
        for g in range(n_groups):
            qg = (qv[g * LANES:(g + 1) * LANES, t * T:(t + 1) * T] * scale).astype(BF16)
            qa, qb = jnp.where(first, qg, zero), jnp.where(first, zero, qg)
            half = T // 2
            qbd_ref[0, t, g] = jnp.concatenate([qa[:, :half], qb[:, :half], qa[:, half:], qb[:, half:]], axis=1)
            vt_ref[0, t, g] = qv[nq + g * LANES:nq + (g + 1) * LANES, t * T:(t + 1) * T].astype(BF16)
    for src, dst in zip(cast_in, cast_out):
        dst[...] = src[...].astype(BF16)


def _proj(x, g, w_in, qk_w, tiles_per_step, layer, cast_weights):
    B, S, D = x.shape
    T = ATT_TILE
    nt = S // T
    tps = tiles_per_step
    steps_per_batch = nt // tps
    n_steps = B * steps_per_batch
    n_groups = 2 * qk_w // LANES
    w_block = lambda col: pl.BlockSpec((None, D, qk_w), lambda b, i: (layer, 0, col), pipeline_mode=pl.Buffered(1))
    cast_in_specs, cast_out_specs, cast_out_shapes = [], [], []
    for w, width, col, hold in cast_weights:
        n_rows = w.shape[1]
        rows = n_rows * hold // n_steps
        assert rows * n_steps == n_rows * hold and rows % BF16_ROWS == 0 and w.shape[2] % width == 0
        cast_in_specs.append(pl.BlockSpec(
            (None, rows, width), lambda b, i, col=col, hold=hold: (layer, (b * steps_per_batch + i) // hold, col)))
        cast_out_specs.append(pl.BlockSpec(
            (rows, width), lambda b, i, hold=hold: ((b * steps_per_batch + i) // hold, 0)))
        cast_out_shapes.append(jax.ShapeDtypeStruct((n_rows, width), BF16))
    kern = functools.partial(_proj_kernel, n_groups=n_groups, scale=HEAD_DIM ** -0.5 * LOG2E,
                             n_cast=len(cast_weights))
    outs = pl.pallas_call(
        kern,
        grid=(B, steps_per_batch),
        in_specs=[
            pl.BlockSpec((1, tps * T, D), lambda b, i: (b, i, 0)),
            pl.BlockSpec((1, D), lambda b, i: (0, 0)),
        ] + [w_block(col) for col in range(6)] + cast_in_specs,
        out_specs=[
            pl.BlockSpec((1, n_groups, tps * T, LANES), lambda b, i: (b, 0, i, 0)),
            pl.BlockSpec((1, tps, n_groups, LANES, 2 * T), lambda b, i: (b, i, 0, 0, 0)),
            pl.BlockSpec((1, tps, n_groups, LANES, T), lambda b, i: (b, i, 0, 0, 0)),
        ] + cast_out_specs,
        out_shape=[
            jax.ShapeDtypeStruct((B, n_groups, S, LANES), BF16),
            jax.ShapeDtypeStruct((B, nt, n_groups, LANES, 2 * T), BF16),
            jax.ShapeDtypeStruct((B, nt, n_groups, LANES, T), BF16),
        ] + cast_out_shapes,
        scratch_shapes=[pltpu.VMEM((D, 2 * qk_w), BF16), pltpu.VMEM((4 * qk_w, D), BF16)],
        compiler_params=pltpu.CompilerParams(
            dimension_semantics=("arbitrary", "arbitrary"), vmem_limit_bytes=VMEM_LIMIT),
        name="proj",
    )(x, g, *([w_in] * 6), *[w for w, _, _, _ in cast_weights])
    return outs[:3], outs[3:]


DIFF_BATCH_BLOCK = 2
SB_BATCH_BLOCK = 1


def _units(k_ref, n_groups):
    return [(bb, g) for bb in range(k_ref.shape[0]) for g in range(n_groups)]


def _key_tile(k_ref, bb, g, j):
    T = ATT_TILE
    start = j * T if isinstance(j, int) else pl.multiple_of(j * T, T)
    return k_ref[bb, g, pl.ds(start, T), :]


def _attention_specs(n_groups, group_block, S, nb):
    T = ATT_TILE
    gb = group_block
    return [
        pl.BlockSpec((nb, 1, n_groups, LANES, 2 * T), lambda b, i: (b, i, gb, 0, 0)),
        pl.BlockSpec((nb, n_groups, S, LANES), lambda b, i: (b, gb, 0, 0)),
        pl.BlockSpec((nb, S // T, n_groups, LANES, T), lambda b, i: (b, 0, gb, 0, 0)),
    ]


def _diff_kernel(lq1_ref, lk1_ref, lq2_ref, lk2_ref, gs_ref, nbt_ref, nbb_ref, nbp_ref, qbd_ref, k_ref, vt_ref,
                 y_ref, z_ref, mx_ref, l_ref, ot_ref, *, n_groups, lambda_init):
    T = ATT_TILE
    P = T // 2
    qi = pl.program_id(1)
    units = list(enumerate(_units(k_ref, n_groups)))

    def score_unit(u, bb, g, j, slot, nb_ref):
        st = _dot(_key_tile(k_ref, bb, g, j), qbd_ref[bb, 0, g])
        z_ref[slot, u] = st if nb_ref is None else st + nb_ref[g]

    def tile(j, slot, next_slot=None):
        ps, alphas = [], []
        for u, _ in units:
            s = z_ref[slot, u]
            m_old = mx_ref[u]
            m = jnp.maximum(m_old, jnp.max(s, axis=0, keepdims=True))
            alphas.append(jnp.exp2(m_old - m))
            p = jnp.exp2(s - m)
            l_ref[u] = alphas[u] * l_ref[u] + jnp.sum(p, axis=0, keepdims=True)
            mx_ref[u] = m
            ps.append(p.astype(BF16))
        for u, (bb, g) in units:
            if next_slot is not None:
                score_unit(u, bb, g, jnp.maximum(j - 1, 0), next_slot, None)
            ot_ref[u] = alphas[u] * ot_ref[u] + _dot(vt_ref[bb, j, g], ps[u])

    def diagonal(with_previous, prefetch):
        for u, (bb, g) in units:
            q = qbd_ref[bb, 0, g]
            kt = _key_tile(k_ref, bb, g, qi)
            z_ref[0, u, :P, :] = _dot(kt[:P], q) + nbt_ref[g]
            z_ref[0, u, P:, T:] = _dot(kt[P:], q[:, T:]) + nbb_ref[g]
            if with_previous:
                score_unit(u, bb, g, qi - 1, 1, nbp_ref)
        w_lo, w_hi = [], []
        for u, _ in units:
            s_t, s_b = z_ref[0, u, :P, :], z_ref[0, u, P:, T:]
            m = jnp.max(s_t, axis=0, keepdims=True)
            if with_previous:
                s_p = z_ref[1, u]
                m = jnp.maximum(m, jnp.max(s_p, axis=0, keepdims=True))
            m = jnp.concatenate([m[:, :T], jnp.maximum(m[:, T:], jnp.max(s_b, axis=0, keepdims=True))], axis=1)
            p_t, p_b = jnp.exp2(s_t - m), jnp.exp2(s_b - m[:, T:])
            lsum = jnp.sum(p_t, axis=0, keepdims=True)
            lsum = jnp.concatenate([lsum[:, :T], lsum[:, T:] + jnp.sum(p_b, axis=0, keepdims=True)], axis=1)
            lo, hi = [p_t[:, :T].astype(BF16)], [p_t[:, T:].astype(BF16), p_b.astype(BF16)]
            if with_previous:
                p_p = jnp.exp2(s_p - m)
                lsum = lsum + jnp.sum(p_p, axis=0, keepdims=True)
                lo, hi = [p_p[:, :T].astype(BF16)] + lo, [p_p[:, T:].astype(BF16)] + hi
            l_ref[u] = lsum
            mx_ref[u] = m
            w_lo.append(jnp.concatenate(lo, axis=0))
            w_hi.append(jnp.concatenate(hi, axis=0))
        for u, (bb, g) in units:
            if prefetch:
                score_unit(u, bb, g, qi - 2, 2, None)
            vd = vt_ref[bb, qi, g]
            v_lo, v_hi = [vd[:, :P]], [vd]
            if with_previous:
                vp = vt_ref[bb, qi - 1, g]
                v_lo, v_hi = [vp] + v_lo, [vp] + v_hi
            ot_ref[u] = jnp.concatenate([_dot(jnp.concatenate(v_lo, axis=1), w_lo[u]),
                                         _dot(jnp.concatenate(v_hi, axis=1), w_hi[u])], axis=1)

    @pl.when(qi == 0)
    def _():
        diagonal(with_previous=False, prefetch=False)

    @pl.when(qi == 1)
    def _():
        diagonal(with_previous=True, prefetch=False)

    @pl.when(qi > 1)
    def _():
        diagonal(with_previous=True, prefetch=True)

    rest = jnp.maximum(qi - 1, 0)
    even = jnp.bitwise_and(rest, 1) == 0
    ends_in_pair = jnp.logical_and(even, rest > 0)

    def pair(t, carry):
        j = qi - 2 - 2 * t
        tile(j, 2, next_slot=0)
        tile(j - 1, 0, next_slot=2)
        return carry
    lax.fori_loop(0, lax.shift_right_logical(rest, 1) - ends_in_pair.astype(jnp.int32), pair, 0)

    @pl.when(ends_in_pair)
    def _():
        tile(1, 2, next_slot=0)
        tile(0, 0)

    @pl.when(jnp.logical_not(even))
    def _():
        tile(0, 2)

    lam = (jnp.exp(jnp.sum(lq1_ref[...] * lk1_ref[...], keepdims=True))
           - jnp.exp(jnp.sum(lq2_ref[...] * lk2_ref[...], keepdims=True)) + lambda_init)
    for u, (bb, g) in units:
        ot = ot_ref[u] * (1.0 / l_ref[u])
        o = (jnp.concatenate([ot[:, :P], ot[:, T:T + P]], axis=1)
             - lam * jnp.concatenate([ot[:, P:T], ot[:, T + P:]], axis=1))
        o = o * lax.rsqrt(jnp.mean(o * o, axis=0, keepdims=True) + NORM_EPS)
        y = o.T * gs_ref[...] * (1.0 - lambda_init)
        y_ref[bb, :, g * LANES:(g + 1) * LANES] = y.astype(y_ref.dtype)


def _diff_attention(qbd, k, vt, near_tables, lq1, lk1, lq2, lk2, gs, lambda_init, n_groups, group_block):
    B, _, S, _ = k.shape
    T = ATT_TILE
    nt = S // T
    nb = DIFF_BATCH_BLOCK
    n_units = nb * n_groups
    kern = functools.partial(_diff_kernel, n_groups=n_groups, lambda_init=lambda_init)
    small = lambda a: pl.BlockSpec(a.shape, lambda b, i: (0,) * a.ndim)
    return pl.pallas_call(
        kern,
        grid=(B // nb, nt),
        in_specs=[small(a) for a in (lq1, lk1, lq2, lk2, gs) + tuple(near_tables)]
        + _attention_specs(n_groups, group_block, S, nb),
        out_specs=pl.BlockSpec((nb, T, n_groups * LANES), lambda b, i: (b, i, 0)),
        out_shape=jax.ShapeDtypeStruct((B, S, n_groups * LANES), BF16),
        scratch_shapes=[
            pltpu.VMEM((3, n_units, T, 2 * T), F32),
            pltpu.VMEM((n_units, 1, 2 * T), F32),
            pltpu.VMEM((n_units, 1, 2 * T), F32),
            pltpu.VMEM((n_units, LANES, 2 * T), F32),
        ],
        compiler_params=pltpu.CompilerParams(
            dimension_semantics=("arbitrary", "arbitrary"), vmem_limit_bytes=VMEM_LIMIT),
        name="diff_attn",
    )(lq1, lk1, lq2, lk2, gs, *near_tables, qbd, k, vt)


def _sb_kernel(nu_ref, qbd_ref, k_ref, vt_ref, y_ref, z_ref, c_ref, ot_ref, *, n_groups):
    T = ATT_TILE
    P = T // 2
    qi = pl.program_id(1)
    units = list(enumerate(_units(k_ref, n_groups)))
    kk = lax.broadcasted_iota(jnp.int32, (P, T), 0)
    qq = lax.broadcasted_iota(jnp.int32, (P, T), 1)
    before = kk < jnp.where(qq >= P, qq - P, qq)
    sign = jnp.uint32(0x80000000)

    def score_unit(u, bb, g, j, slot):
        z_ref[slot, u] = _dot(_key_tile(k_ref, bb, g, j), qbd_ref[bb, 0, g])

    def score_tile(j, slot):
        for u, (bb, g) in units:
            score_unit(u, bb, g, j, slot)

    def softplus2(z):
        neg_abs = lax.bitcast_convert_type(lax.bitcast_convert_type(z, jnp.uint32) | sign, F32)
        return jnp.maximum(z, 0.0) + jnp.log2(1.0 + jnp.exp2(neg_abs))

    def tile(j, slot, prefetch):
        sps = [softplus2(z_ref[slot, u]).astype(BF16) for u, _ in units]
        incls = []
        for u, (bb, g) in units:
            if prefetch:
                score_unit(u, bb, g, jnp.maximum(j - 1, 0), 1 - slot)
            incls.append(_dot(nu_ref[...], sps[u]))
        ws = []
        for u, _ in units:
            ws.append(jnp.exp2((z_ref[slot, u] + c_ref[u]) + incls[u]).astype(BF16))
            c_ref[u] += incls[u][0:1]
        for u, (bb, g) in units:
            ot_ref[u] += _dot(vt_ref[bb, j, g], ws[u])

    def diagonal(with_previous):
        for u, (bb, g) in units:
            q = qbd_ref[bb, 0, g]
            kt = _key_tile(k_ref, bb, g, qi)
            z_ref[0, u, :P, :] = _dot(kt[:P], q)
            z_ref[0, u, P:, T:] = _dot(kt[P:], q[:, T:])
        sp_lo, sp_hi = [], []
        for u, _ in units:
            st = softplus2(z_ref[0, u, :P, :])
            sb = jnp.where(before, softplus2(z_ref[0, u, P:, T:]), 0.0)
            sp_lo.append(jnp.where(before, st[:, :T], 0.0).astype(BF16))
            sp_hi.append(jnp.concatenate([st[:, T:], sb], axis=0).astype(BF16))
        incl_lo, incl_hi = [], []
        for u, (bb, g) in units:
            if with_previous:
                score_unit(u, bb, g, qi - 1, 1)
            incl_lo.append(_dot(nu_ref[:P, :P], sp_lo[u]))
            incl_hi.append(_dot(nu_ref[...], sp_hi[u]))
        if with_previous:
            sp_p = [softplus2(z_ref[1, u]).astype(BF16) for u, _ in units]
            incl_p = [_dot(nu_ref[...], sp_p[u]) for u, _ in units]
        for u, (bb, g) in units:
            zt, zb = z_ref[0, u, :P, :], z_ref[0, u, P:, T:]
            w_lo = jnp.where(before, jnp.exp2(zt[:, :T] + incl_lo[u]), 0.0).astype(BF16)
            w_hi = jnp.concatenate([jnp.exp2(zt[:, T:] + incl_hi[u][:P]),
                                    jnp.where(before, jnp.exp2(zb + incl_hi[u][P:]), 0.0)], axis=0).astype(BF16)
            c = jnp.concatenate([incl_lo[u][0:1], incl_hi[u][0:1]], axis=1)
            vd = vt_ref[bb, qi, g]
            pv = jnp.concatenate([_dot(vd[:, :P], w_lo), _dot(vd, w_hi)], axis=1)
            if with_previous:
                w_p = jnp.exp2((z_ref[1, u] + c) + incl_p[u]).astype(BF16)
                pv = pv + _dot(vt_ref[bb, qi - 1, g], w_p)
                c = c + incl_p[u][0:1]
            ot_ref[u] = pv
            c_ref[u] = c

    @pl.when(qi == 0)
    def _():
        diagonal(False)

    @pl.when(qi > 0)
    def _():
        diagonal(True)

    def alive():
        return jnp.max(c_ref[...]) >= DEAD_LOG2

    rest = jnp.maximum(qi - 1, 0)
    n_pairs = lax.shift_right_logical(rest, 1)

    def pair(carry):
        t, _ = carry
        j = qi - 2 - 2 * t
        score_tile(j, 0)
        tile(j, 0, prefetch=True)
        tile(j - 1, 1, prefetch=False)
        return t + 1, alive()
    _, go = lax.while_loop(lambda carry: jnp.logical_and(carry[0] < n_pairs, carry[1]), pair,
                           (jnp.int32(0), alive()))

    @pl.when(jnp.logical_and(jnp.bitwise_and(rest, 1) == 1, go))
    def _():
        score_tile(0, 0)
        tile(0, 0, prefetch=False)

    for u, (bb, g) in units:
        ot = ot_ref[u]
        o = jnp.concatenate([jnp.concatenate([ot[:HEAD_DIM, :P], ot[:HEAD_DIM, T:T + P]], axis=1),
                             jnp.concatenate([ot[HEAD_DIM:, P:T], ot[HEAD_DIM:, T + P:]], axis=1)], axis=0)
        y_ref[bb, :, g * LANES:(g + 1) * LANES] = o.T.astype(y_ref.dtype)


def _sb_attention(qbd, k, vt, nu, n_groups, group_block):
    B, _, S, _ = k.shape
    T = ATT_TILE
    nb = SB_BATCH_BLOCK
    n_units = nb * n_groups
    kern = functools.partial(_sb_kernel, n_groups=n_groups)
    return pl.pallas_call(
        kern,
        grid=(B // nb, S // T),
        in_specs=[pl.BlockSpec(nu.shape, lambda b, i: (0, 0))] + _attention_specs(n_groups, group_block, S, nb),
        out_specs=pl.BlockSpec((nb, T, n_groups * LANES), lambda b, i: (b, i, 0)),
        out_shape=jax.ShapeDtypeStruct((B, S, n_groups * LANES), BF16),
        scratch_shapes=[
            pltpu.VMEM((2, n_units, T, 2 * T), F32),
            pltpu.VMEM((n_units, 1, 2 * T), F32),
            pltpu.VMEM((n_units, LANES, 2 * T), F32),
        ],
        compiler_params=pltpu.CompilerParams(
            dimension_semantics=("arbitrary", "arbitrary"), vmem_limit_bytes=VMEM_LIMIT),
        name="sb_attn",
    )(nu, qbd, k, vt)


def _mix_kernel(x_ref, yd_ref, ys_ref, g_ref, wgd_ref, wgs_ref, wbd_ref, wbs_ref, wo_ref, o_ref):
    x = x_ref[...]
    h = _rms(x, g_ref[...]).astype(BF16)
    merged = (jax.nn.sigmoid(_dot(h, wgd_ref[...])) * _dot(yd_ref[...], wbd_ref[...])
              + jax.nn.sigmoid(_dot(h, wgs_ref[...])) * _dot(ys_ref[...], wbs_ref[...]))
    o_ref[...] = x + _dot(merged.astype(BF16), wo_ref[...])


def _mix(x2d, yd, ys, g, wgd, wgs, wbd, wbs, wo, tm):
    N, D = x2d.shape
    const = lambda a: pl.BlockSpec(a.shape, lambda i: (0, 0))
    row = lambda w: pl.BlockSpec((tm, w), lambda i: (i, 0))
    return pl.pallas_call(
        _mix_kernel,
        grid=(N // tm,),
        in_specs=[row(D), row(yd.shape[1]), row(ys.shape[1]),
                  const(g), const(wgd), const(wgs), const(wbd), const(wbs), const(wo)],
        out_specs=row(D),
        out_shape=jax.ShapeDtypeStruct((N, D), F32),
        compiler_params=pltpu.CompilerParams(dimension_semantics=("arbitrary",), vmem_limit_bytes=VMEM_LIMIT),
        name="mix",
    )(x2d, yd, ys, g, wgd, wgs, wbd, wbs, wo)


def _ffn_kernel(x_ref, g_ref, wup_ref, cw_ref, cb_ref, wdn_ref, gf_ref, o_ref, abuf_ref, *, final_norm):
    tm = x_ref.shape[1]
    F = cw_ref.shape[1]
    H = SUBLANES

    @pl.when(pl.program_id(1) == 0)
    def _():
        abuf_ref[0:H] = jnp.zeros((H, F), F32)

    x = x_ref[0]
    h = _rms(x, g_ref[...]).astype(BF16)
    up = _dot(h, wup_ref[...])
    a = up[:, :F]
    abuf_ref[H:H + tm] = a
    conv = (cw_ref[0:1] * abuf_ref[H - 2:H - 2 + tm] + cw_ref[1:2] * abuf_ref[H - 1:H - 1 + tm]
            + cw_ref[2:3] * a + cb_ref[...])
    abuf_ref[0:H] = abuf_ref[tm:tm + H]
    gelu = 0.5 * conv * (1.0 + lax.erf(conv * (2.0 ** -0.5)))
    act = gelu * up[:, F:]
    x = x + _dot(act.astype(BF16), wdn_ref[...])
    o_ref[0] = _rms(x, gf_ref[...]) if final_norm else x


def _ffn(x, g, wup, cw, cb, wdn, gf, tm, final_norm):
    B, S, D = x.shape
    F = cw.shape[1]
    const = lambda a: pl.BlockSpec(a.shape, lambda b, i: (0, 0), pipeline_mode=pl.Buffered(1))
    return pl.pallas_call(
        functools.partial(_ffn_kernel, final_norm=final_norm),
        grid=(B, S // tm),
        in_specs=[pl.BlockSpec((1, tm, D), lambda b, i: (b, i, 0)),
                  const(g), const(wup), const(cw), const(cb), const(wdn), const(gf)],
        out_specs=pl.BlockSpec((1, tm, D), lambda b, i: (b, i, 0)),
        out_shape=jax.ShapeDtypeStruct((B, S, D), F32),
        scratch_shapes=[pltpu.VMEM((tm + SUBLANES, F), F32)],
        compiler_params=pltpu.CompilerParams(
            dimension_semantics=("arbitrary", "arbitrary"), vmem_limit_bytes=VMEM_LIMIT),
        name="ffn",
    )(x, g, wup, cw, cb, wdn, gf)


def _t5_bucket(rel):
    half = N_BUCKETS // 2
    ret = jnp.where(rel > 0, half, 0)
    n = jnp.abs(rel)
    max_exact = half // 2
    nf = jnp.maximum(n, 1).astype(jnp.float32)
    large = max_exact + (jnp.log(nf / max_exact) / math.log(MAX_DISTANCE / max_exact)
                         * (half - max_exact)).astype(jnp.int32)
    large = jnp.minimum(large, half - 1)
    return ret + jnp.where(n < max_exact, n, large)


def _near_bias_tables(rel_bias):
    T = ATT_TILE
    L = 2 * T
    H = rel_bias.shape[1]
    far = rel_bias[_t5_bucket(jnp.asarray(-MAX_DISTANCE))].astype(F32)
    m = jnp.arange(L)
    rel = jnp.where(m < T, -m, L - m)

    def toeplitz(rels):
        v = ((rel_bias[_t5_bucket(rels)].astype(F32) - far) * LOG2E).T
        rows = jnp.tile(v, (1, T))[:, :T * (L - 1)].reshape(H, T, L - 1)
        return rows[:, :, :T]

    kpos = jnp.arange(T)[:, None]
    qpos = jnp.arange(T)[None, :]
    visible = (kpos // CHUNK) <= (qpos // CHUNK)
    diag, prev = jnp.where(visible[None], toeplitz(rel), -jnp.inf), toeplitz(rel - T)
    P = T // 2
    cols = lambda t: jnp.concatenate([t[..., :P], t[..., :P], t[..., P:], t[..., P:]], axis=-1)
    return cols(diag[:, :P]), jnp.concatenate([diag[:, P:, P:], diag[:, P:, P:]], axis=-1), cols(prev)


def kernel(x, norm_mix_g, w_in, diff_lambda_q1, diff_lambda_k1, diff_lambda_q2, diff_lambda_k2, diff_subln_g,
           rel_bias, w_branch_diff, w_branch_sb, w_out, norm_ffn_g, w_ffn_up, ffn_conv_w, ffn_conv_b, w_ffn_down,
           norm_final_g):
    B, S, D = x.shape
    depth = w_in.shape[0]
    T = ATT_TILE
    n_heads_diff = rel_bias.shape[1]
    qk_w = n_heads_diff * 2 * HEAD_DIM
    ngd = qk_w // LANES
    assert S % T == 0 and (T // 2) % CHUNK == 0 and T >= MAX_DISTANCE and B % DIFF_BATCH_BLOCK == 0
    assert (6 * qk_w) % D == 0
    d_ff = ffn_conv_w.shape[-1]

    near_tables = _near_bias_tables(rel_bias)
    neg_upper = -(jnp.arange(T)[None, :] >= jnp.arange(T)[:, None]).astype(BF16)

    for layer in range(depth):
        lambda_init = 0.8 - 0.6 * math.exp(-0.3 * layer)
        gate_col = 6 * qk_w // D
        cast_weights = [(w_in, D, gate_col, 1), (w_in, D, gate_col + 1, 1), (w_branch_diff, D, 0, 1),
                        (w_branch_sb, D, 0, 1), (w_out, D, 0, 1), (w_ffn_up, 2 * d_ff, 0, 1),
                        (w_ffn_down, D, 0, 2)]
        (k, qbd, vt), (wgd, wgs, wbd, wbs, wo, wup, wdn) = _proj(
            x, norm_mix_g[layer][None], w_in, qk_w, tiles_per_step=2, layer=layer, cast_weights=cast_weights)

        row = lambda a: a[layer][None].astype(F32)
        y_diff = _diff_attention(qbd, k, vt, near_tables, row(diff_lambda_q1), row(diff_lambda_k1),
                                 row(diff_lambda_q2), row(diff_lambda_k2), row(diff_subln_g), lambda_init, ngd, 0)
        y_sb = _sb_attention(qbd, k, vt, neg_upper, ngd, 1)

        x = _mix(x.reshape(B * S, D), y_diff.reshape(B * S, -1), y_sb.reshape(B * S, -1), norm_mix_g[layer][None],
                 wgd, wgs, wbd, wbs, wo, tm=512).reshape(B, S, D)
        x = _ffn(x, norm_ffn_g[layer][None], wup, ffn_conv_w[layer], ffn_conv_b[layer][None], wdn,
                 norm_final_g[None], tm=512, final_norm=layer == depth - 1)
    return x
```

```python
import functools
import math

import jax
import jax.numpy as jnp
from jax import lax
from jax.experimental import pallas as pl
from jax.experimental.pallas import tpu as pltpu

F32 = jnp.float32
BF16 = jnp.bfloat16

HEAD_DIM = 64
CHUNK = 64
N_BUCKETS = 32
MAX_DISTANCE = 128
NORM_EPS = 1e-6
LOG2E = math.log2(math.e)
DEAD_LOG2 = -150.0

LANES = 128
SUBLANES = 8
BF16_ROWS = 16
ATT_TILE = 256
VMEM_LIMIT = 56 * 1024 * 1024


def _dot(a, b):
    return jnp.dot(a, b, preferred_element_type=F32)


def _rms(x, g):
    return x * lax.rsqrt(jnp.mean(x * x, axis=-1, keepdims=True) + NORM_EPS) * g


def _proj_kernel(x_ref, g_ref, dq_ref, dk_ref, dv_ref, sq_ref, sk_ref, sv_ref, *refs, n_groups, scale, n_cast):
    cast_in, (k_ref, qbd_ref, vt_ref), cast_out = refs[:n_cast], refs[n_cast:n_cast + 3], refs[n_cast + 3:n_cast * 2 + 3]
    wk_ref, wqvt_ref = refs[n_cast * 2 + 3:]
    T = ATT_TILE
    tiles = x_ref.shape[1] // T

    @pl.when(jnp.logical_and(pl.program_id(0) == 0, pl.program_id(1) == 0))
    def _():
        width = dk_ref.shape[1]
        for n, ref in enumerate((dk_ref, sk_ref)):
            wk_ref[:, n * width:(n + 1) * width] = ref[...].astype(BF16)
        for n, ref in enumerate((dq_ref, sq_ref, dv_ref, sv_ref)):
            wqvt_ref[n * width:(n + 1) * width, :] = ref[...].T.astype(BF16)

    h = _rms(x_ref[0], g_ref[...]).astype(BF16)
    kk = _dot(h, wk_ref[...]).astype(BF16)
    for g in range(n_groups):
        k_ref[0, g] = kk[:, g * LANES:(g + 1) * LANES]
    qv = lax.dot_general(wqvt_ref[...], h, (((1,), (1,)), ((), ())), preferred_element_type=F32)
    nq = n_groups * LANES
    first = lax.broadcasted_iota(jnp.int32, (LANES, T), 0) < HEAD_DIM
    zero = jnp.zeros((LANES, T), BF16)
    for t in range(tiles):
        for g in range(n_groups):
            qg = (qv[g * LANES:(g + 1) * LANES, t * T:(t + 1) * T] * scale).astype(BF16)
            qa, qb = jnp.where(first, qg, zero), jnp.where(first, zero, qg)
            if g < n_groups // 2:
                qbd_ref[0, t, g] = jnp.concatenate([qa, qb], axis=1)
            else:
                half = T // 2
                qbd_ref[0, t, g] = jnp.concatenate([qa[:, :half], qb[:, :half], qa[:, half:], qb[:, half:]], axis=1)
            vt_ref[0, t, g] = qv[nq + g * LANES:nq + (g + 1) * LANES, t * T:(t + 1) * T].astype(BF16)
    for src, dst in zip(cast_in, cast_out):
        dst[...] = src[...].astype(BF16)


def _proj(x, g, w_in, qk_w, tiles_per_step, layer, cast_weights):
    B, S, D = x.shape
    T = ATT_TILE
    nt = S // T
    tps = tiles_per_step
    steps_per_batch = nt // tps
    n_steps = B * steps_per_batch
    n_groups = 2 * qk_w // LANES
    w_block = lambda col: pl.BlockSpec((None, D, qk_w), lambda b, i: (layer, 0, col), pipeline_mode=pl.Buffered(1))
    cast_in_specs, cast_out_specs, cast_out_shapes = [], [], []
    for w, width, col, hold in cast_weights:
        n_rows = w.shape[1]
        rows = n_rows * hold // n_steps
        assert rows * n_steps == n_rows * hold and rows % BF16_ROWS == 0 and w.shape[2] % width == 0
        cast_in_specs.append(pl.BlockSpec(
            (None, rows, width), lambda b, i, col=col, hold=hold: (layer, (b * steps_per_batch + i) // hold, col)))
        cast_out_specs.append(pl.BlockSpec(
            (rows, width), lambda b, i, hold=hold: ((b * steps_per_batch + i) // hold, 0)))
        cast_out_shapes.append(jax.ShapeDtypeStruct((n_rows, width), BF16))
    kern = functools.partial(_proj_kernel, n_groups=n_groups, scale=HEAD_DIM ** -0.5 * LOG2E,
                             n_cast=len(cast_weights))
    outs = pl.pallas_call(
        kern,
        grid=(B, steps_per_batch),
        in_specs=[
            pl.BlockSpec((1, tps * T, D), lambda b, i: (b, i, 0)),
            pl.BlockSpec((1, D), lambda b, i: (0, 0)),
        ] + [w_block(col) for col in range(6)] + cast_in_specs,
        out_specs=[
            pl.BlockSpec((1, n_groups, tps * T, LANES), lambda b, i: (b, 0, i, 0)),
            pl.BlockSpec((1, tps, n_groups, LANES, 2 * T), lambda b, i: (b, i, 0, 0, 0)),
            pl.BlockSpec((1, tps, n_groups, LANES, T), lambda b, i: (b, i, 0, 0, 0)),
        ] + cast_out_specs,
        out_shape=[
            jax.ShapeDtypeStruct((B, n_groups, S, LANES), BF16),
            jax.ShapeDtypeStruct((B, nt, n_groups, LANES, 2 * T), BF16),
            jax.ShapeDtypeStruct((B, nt, n_groups, LANES, T), BF16),
        ] + cast_out_shapes,
        scratch_shapes=[pltpu.VMEM((D, 2 * qk_w), BF16), pltpu.VMEM((4 * qk_w, D), BF16)],
        compiler_params=pltpu.CompilerParams(
            dimension_semantics=("arbitrary", "arbitrary"), vmem_limit_bytes=VMEM_LIMIT),
        name="proj",
    )(x, g, *([w_in] * 6), *[w for w, _, _, _ in cast_weights])
    return outs[:3], outs[3:]


DIFF_BATCH_BLOCK = 2
SB_BATCH_BLOCK = 1


def _units(k_ref, n_groups):
    return [(bb, g) for bb in range(k_ref.shape[0]) for g in range(n_groups)]


def _key_tile(k_ref, bb, g, j):
    T = ATT_TILE
    start = j * T if isinstance(j, int) else pl.multiple_of(j * T, T)
    return k_ref[bb, g, pl.ds(start, T), :]


def _attention_specs(n_groups, group_block, S, nb):
    T = ATT_TILE
    gb = group_block
    return [
        pl.BlockSpec((nb, 1, n_groups, LANES, 2 * T), lambda b, i: (b, i, gb, 0, 0)),
        pl.BlockSpec((nb, n_groups, S, LANES), lambda b, i: (b, gb, 0, 0)),
        pl.BlockSpec((nb, S // T, n_groups, LANES, T), lambda b, i: (b, 0, gb, 0, 0)),
    ]


def _diff_kernel(lq1_ref, lk1_ref, lq2_ref, lk2_ref, gs_ref, nbd_ref, nbp_ref, qbd_ref, k_ref, vt_ref,
                 y_ref, z_ref, mx_ref, l_ref, ot_ref, *, n_groups, lambda_init):
    T = ATT_TILE
    qi = pl.program_id(1)
    units = list(enumerate(_units(k_ref, n_groups)))

    def score_unit(u, bb, g, j, slot, nb_ref):
        st = _dot(_key_tile(k_ref, bb, g, j), qbd_ref[bb, 0, g])
        if nb_ref is not None:
            nb = nb_ref[g]
            st = jnp.concatenate([st[:, :T] + nb, st[:, T:] + nb], axis=1)
        z_ref[slot, u] = st

    def tile(j, slot, next_slot=None, first=False):
        ps, alphas = [], []
        for u, _ in units:
            s = z_ref[slot, u]
            m = jnp.max(s, axis=0, keepdims=True)
            if not first:
                m_old = mx_ref[u]
                m = jnp.maximum(m_old, m)
                alphas.append(jnp.exp2(m_old - m))
            p = jnp.exp2(s - m)
            lsum = jnp.sum(p, axis=0, keepdims=True)
            l_ref[u] = lsum if first else alphas[u] * l_ref[u] + lsum
            mx_ref[u] = m
            ps.append(p.astype(BF16))
        for u, (bb, g) in units:
            if next_slot is not None:
                score_unit(u, bb, g, jnp.maximum(j - 1, 0), next_slot, None)
            pv = _dot(vt_ref[bb, j, g], ps[u])
            ot_ref[u] = pv if first else alphas[u] * ot_ref[u] + pv

    def diagonal_pair(prefetch):
        for u, (bb, g) in units:
            score_unit(u, bb, g, qi, 0, nbd_ref)
            score_unit(u, bb, g, qi - 1, 1, nbp_ref)
        ps = []
        for u, _ in units:
            s_d, s_p = z_ref[0, u], z_ref[1, u]
            m = jnp.maximum(jnp.max(s_d, axis=0, keepdims=True), jnp.max(s_p, axis=0, keepdims=True))
            p_d, p_p = jnp.exp2(s_d - m), jnp.exp2(s_p - m)
            l_ref[u] = jnp.sum(p_d, axis=0, keepdims=True) + jnp.sum(p_p, axis=0, keepdims=True)
            mx_ref[u] = m
            ps.append(jnp.concatenate([p_p.astype(BF16), p_d.astype(BF16)], axis=0))
        for u, (bb, g) in units:
            if prefetch:
                score_unit(u, bb, g, qi - 2, 2, None)
            v = jnp.concatenate([vt_ref[bb, qi - 1, g], vt_ref[bb, qi, g]], axis=1)
            ot_ref[u] = _dot(v, ps[u])

    @pl.when(qi == 0)
    def _():
        for u, (bb, g) in units:
            score_unit(u, bb, g, 0, 0, nbd_ref)
        tile(0, 0, first=True)

    @pl.when(qi == 1)
    def _():
        diagonal_pair(prefetch=False)

    @pl.when(qi > 1)
    def _():
        diagonal_pair(prefetch=True)

    rest = jnp.maximum(qi - 1, 0)
    even = jnp.bitwise_and(rest, 1) == 0
    ends_in_pair = jnp.logical_and(even, rest > 0)

    def pair(t, carry):
        j = qi - 2 - 2 * t
        tile(j, 2, next_slot=0)
        tile(j - 1, 0, next_slot=2)
        return carry
    lax.fori_loop(0, lax.shift_right_logical(rest, 1) - ends_in_pair.astype(jnp.int32), pair, 0)

    @pl.when(ends_in_pair)
    def _():
        tile(1, 2, next_slot=0)
        tile(0, 0)

    @pl.when(jnp.logical_not(even))
    def _():
        tile(0, 2)

    lam = (jnp.exp(jnp.sum(lq1_ref[...] * lk1_ref[...], keepdims=True))
           - jnp.exp(jnp.sum(lq2_ref[...] * lk2_ref[...], keepdims=True)) + lambda_init)
    for u, (bb, g) in units:
        ot = ot_ref[u] * (1.0 / l_ref[u])
        o = ot[:, :T] - lam * ot[:, T:]
        o = o * lax.rsqrt(jnp.mean(o * o, axis=0, keepdims=True) + NORM_EPS)
        y = o.T * gs_ref[...] * (1.0 - lambda_init)
        y_ref[bb, :, g * LANES:(g + 1) * LANES] = y.astype(y_ref.dtype)


def _diff_attention(qbd, k, vt, near_diag, near_prev, lq1, lk1, lq2, lk2, gs, lambda_init, n_groups, group_block):
    B, _, S, _ = k.shape
    T = ATT_TILE
    nt = S // T
    nb = DIFF_BATCH_BLOCK
    n_units = nb * n_groups
    kern = functools.partial(_diff_kernel, n_groups=n_groups, lambda_init=lambda_init)
    small = lambda a: pl.BlockSpec(a.shape, lambda b, i: (0,) * a.ndim)
    return pl.pallas_call(
        kern,
        grid=(B // nb, nt),
        in_specs=[small(lq1), small(lk1), small(lq2), small(lk2), small(gs), small(near_diag), small(near_prev)]
        + _attention_specs(n_groups, group_block, S, nb),
        out_specs=pl.BlockSpec((nb, T, n_groups * LANES), lambda b, i: (b, i, 0)),
        out_shape=jax.ShapeDtypeStruct((B, S, n_groups * LANES), BF16),
        scratch_shapes=[
            pltpu.VMEM((3, n_units, T, 2 * T), F32),
            pltpu.VMEM((n_units, 1, 2 * T), F32),
            pltpu.VMEM((n_units, 1, 2 * T), F32),
            pltpu.VMEM((n_units, LANES, 2 * T), F32),
        ],
        compiler_params=pltpu.CompilerParams(
            dimension_semantics=("arbitrary", "arbitrary"), vmem_limit_bytes=VMEM_LIMIT),
        name="diff_attn",
    )(lq1, lk1, lq2, lk2, gs, near_diag, near_prev, qbd, k, vt)


def _sb_kernel(nu_ref, qbd_ref, k_ref, vt_ref, y_ref, z_ref, c_ref, ot_ref, *, n_groups):
    T = ATT_TILE
    P = T // 2
    qi = pl.program_id(1)
    units = list(enumerate(_units(k_ref, n_groups)))
    kk = lax.broadcasted_iota(jnp.int32, (P, T), 0)
    qq = lax.broadcasted_iota(jnp.int32, (P, T), 1)
    before = kk < jnp.where(qq >= P, qq - P, qq)
    sign = jnp.uint32(0x80000000)

    def score_unit(u, bb, g, j, slot):
        z_ref[slot, u] = _dot(_key_tile(k_ref, bb, g, j), qbd_ref[bb, 0, g])

    def score_tile(j, slot):
        for u, (bb, g) in units:
            score_unit(u, bb, g, j, slot)

    def softplus2(z):
        neg_abs = lax.bitcast_convert_type(lax.bitcast_convert_type(z, jnp.uint32) | sign, F32)
        return jnp.maximum(z, 0.0) + jnp.log2(1.0 + jnp.exp2(neg_abs))

    def tile(j, slot, prefetch):
        sps = [softplus2(z_ref[slot, u]).astype(BF16) for u, _ in units]
        incls = []
        for u, (bb, g) in units:
            if prefetch:
                score_unit(u, bb, g, jnp.maximum(j - 1, 0), 1 - slot)
            incls.append(_dot(nu_ref[...], sps[u]))
        ws = []
        for u, _ in units:
            ws.append(jnp.exp2((z_ref[slot, u] + c_ref[u]) + incls[u]).astype(BF16))
            c_ref[u] += incls[u][0:1]
        for u, (bb, g) in units:
            ot_ref[u] += _dot(vt_ref[bb, j, g], ws[u])

    def diagonal(with_previous):
        for u, (bb, g) in units:
            q = qbd_ref[bb, 0, g]
            kt = _key_tile(k_ref, bb, g, qi)
            z_ref[0, u, :P, :] = _dot(kt[:P], q)
            z_ref[0, u, P:, T:] = _dot(kt[P:], q[:, T:])
        sp_lo, sp_hi = [], []
        for u, _ in units:
            st = softplus2(z_ref[0, u, :P, :])
            sb = jnp.where(before, softplus2(z_ref[0, u, P:, T:]), 0.0)
            sp_lo.append(jnp.where(before, st[:, :T], 0.0).astype(BF16))
            sp_hi.append(jnp.concatenate([st[:, T:], sb], axis=0).astype(BF16))
        incl_lo, incl_hi = [], []
        for u, (bb, g) in units:
            if with_previous:
                score_unit(u, bb, g, qi - 1, 1)
            incl_lo.append(_dot(nu_ref[:P, :P], sp_lo[u]))
            incl_hi.append(_dot(nu_ref[...], sp_hi[u]))
        if with_previous:
            sp_p = [softplus2(z_ref[1, u]).astype(BF16) for u, _ in units]
            incl_p = [_dot(nu_ref[...], sp_p[u]) for u, _ in units]
        for u, (bb, g) in units:
            zt, zb = z_ref[0, u, :P, :], z_ref[0, u, P:, T:]
            w_lo = jnp.where(before, jnp.exp2(zt[:, :T] + incl_lo[u]), 0.0).astype(BF16)
            w_hi = jnp.concatenate([jnp.exp2(zt[:, T:] + incl_hi[u][:P]),
                                    jnp.where(before, jnp.exp2(zb + incl_hi[u][P:]), 0.0)], axis=0).astype(BF16)
            c = jnp.concatenate([incl_lo[u][0:1], incl_hi[u][0:1]], axis=1)
            vd = vt_ref[bb, qi, g]
            pv = jnp.concatenate([_dot(vd[:, :P], w_lo), _dot(vd, w_hi)], axis=1)
            if with_previous:
                w_p = jnp.exp2((z_ref[1, u] + c) + incl_p[u]).astype(BF16)
                pv = pv + _dot(vt_ref[bb, qi - 1, g], w_p)
                c = c + incl_p[u][0:1]
            ot_ref[u] = pv
            c_ref[u] = c

    @pl.when(qi == 0)
    def _():
        diagonal(False)

    @pl.when(qi > 0)
    def _():
        diagonal(True)

    def alive():
        return jnp.max(c_ref[...]) >= DEAD_LOG2

    rest = jnp.maximum(qi - 1, 0)
    n_pairs = lax.shift_right_logical(rest, 1)

    def pair(carry):
        t, _ = carry
        j = qi - 2 - 2 * t
        score_tile(j, 0)
        tile(j, 0, prefetch=True)
        tile(j - 1, 1, prefetch=False)
        return t + 1, alive()
    _, go = lax.while_loop(lambda carry: jnp.logical_and(carry[0] < n_pairs, carry[1]), pair,
                           (jnp.int32(0), alive()))

    @pl.when(jnp.logical_and(jnp.bitwise_and(rest, 1) == 1, go))
    def _():
        score_tile(0, 0)
        tile(0, 0, prefetch=False)

    for u, (bb, g) in units:
        ot = ot_ref[u]
        o = jnp.concatenate([jnp.concatenate([ot[:HEAD_DIM, :P], ot[:HEAD_DIM, T:T + P]], axis=1),
                             jnp.concatenate([ot[HEAD_DIM:, P:T], ot[HEAD_DIM:, T + P:]], axis=1)], axis=0)
        y_ref[bb, :, g * LANES:(g + 1) * LANES] = o.T.astype(y_ref.dtype)


def _sb_attention(qbd, k, vt, nu, n_groups, group_block):
    B, _, S, _ = k.shape
    T = ATT_TILE
    nb = SB_BATCH_BLOCK
    n_units = nb * n_groups
    kern = functools.partial(_sb_kernel, n_groups=n_groups)
    return pl.pallas_call(
        kern,
        grid=(B // nb, S // T),
        in_specs=[pl.BlockSpec(nu.shape, lambda b, i: (0, 0))] + _attention_specs(n_groups, group_block, S, nb),
        out_specs=pl.BlockSpec((nb, T, n_groups * LANES), lambda b, i: (b, i, 0)),
        out_shape=jax.ShapeDtypeStruct((B, S, n_groups * LANES), BF16),
        scratch_shapes=[
            pltpu.VMEM((2, n_units, T, 2 * T), F32),
            pltpu.VMEM((n_units, 1, 2 * T), F32),
            pltpu.VMEM((n_units, LANES, 2 * T), F32),
        ],
        compiler_params=pltpu.CompilerParams(
            dimension_semantics=("arbitrary", "arbitrary"), vmem_limit_bytes=VMEM_LIMIT),
        name="sb_attn",
    )(nu, qbd, k, vt)


def _mix_kernel(x_ref, yd_ref, ys_ref, g_ref, wgd_ref, wgs_ref, wbd_ref, wbs_ref, wo_ref, o_ref):
    x = x_ref[...]
    h = _rms(x, g_ref[...]).astype(BF16)
    merged = (jax.nn.sigmoid(_dot(h, wgd_ref[...])) * _dot(yd_ref[...], wbd_ref[...])
              + jax.nn.sigmoid(_dot(h, wgs_ref[...])) * _dot(ys_ref[...], wbs_ref[...]))
    o_ref[...] = x + _dot(merged.astype(BF16), wo_ref[...])


def _mix(x2d, yd, ys, g, wgd, wgs, wbd, wbs, wo, tm):
    N, D = x2d.shape
    const = lambda a: pl.BlockSpec(a.shape, lambda i: (0, 0), pipeline_mode=pl.Buffered(1))
    row = lambda w: pl.BlockSpec((tm, w), lambda i: (i, 0))
    return pl.pallas_call(
        _mix_kernel,
        grid=(N // tm,),
        in_specs=[row(D), row(yd.shape[1]), row(ys.shape[1]),
                  const(g), const(wgd), const(wgs), const(wbd), const(wbs), const(wo)],
        out_specs=row(D),
        out_shape=jax.ShapeDtypeStruct((N, D), F32),
        compiler_params=pltpu.CompilerParams(dimension_semantics=("arbitrary",), vmem_limit_bytes=VMEM_LIMIT),
        name="mix",
    )(x2d, yd, ys, g, wgd, wgs, wbd, wbs, wo)


def _ffn_kernel(x_ref, g_ref, wup_ref, cw_ref, cb_ref, wdn_ref, gf_ref, o_ref, abuf_ref, *, final_norm):
    tm = x_ref.shape[1]
    F = cw_ref.shape[1]
    H = SUBLANES

    @pl.when(pl.program_id(1) == 0)
    def _():
        abuf_ref[0:H] = jnp.zeros((H, F), F32)

    x = x_ref[0]
    h = _rms(x, g_ref[...]).astype(BF16)
    up = _dot(h, wup_ref[...])
    a = up[:, :F]
    abuf_ref[H:H + tm] = a
    conv = (cw_ref[0:1] * abuf_ref[H - 2:H - 2 + tm] + cw_ref[1:2] * abuf_ref[H - 1:H - 1 + tm]
            + cw_ref[2:3] * a + cb_ref[...])
    abuf_ref[0:H] = abuf_ref[tm:tm + H]
    gelu = 0.5 * conv * (1.0 + lax.erf(conv * (2.0 ** -0.5)))
    act = gelu * up[:, F:]
    x = x + _dot(act.astype(BF16), wdn_ref[...])
    o_ref[0] = _rms(x, gf_ref[...]) if final_norm else x


def _ffn(x, g, wup, cw, cb, wdn, gf, tm, final_norm):
    B, S, D = x.shape
    F = cw.shape[1]
    const = lambda a: pl.BlockSpec(a.shape, lambda b, i: (0, 0), pipeline_mode=pl.Buffered(1))
    return pl.pallas_call(
        functools.partial(_ffn_kernel, final_norm=final_norm),
        grid=(B, S // tm),
        in_specs=[pl.BlockSpec((1, tm, D), lambda b, i: (b, i, 0)),
                  const(g), const(wup), const(cw), const(cb), const(wdn), const(gf)],
        out_specs=pl.BlockSpec((1, tm, D), lambda b, i: (b, i, 0)),
        out_shape=jax.ShapeDtypeStruct((B, S, D), F32),
        scratch_shapes=[pltpu.VMEM((tm + SUBLANES, F), F32)],
        compiler_params=pltpu.CompilerParams(
            dimension_semantics=("arbitrary", "arbitrary"), vmem_limit_bytes=VMEM_LIMIT),
        name="ffn",
    )(x, g, wup, cw, cb, wdn, gf)


def _t5_bucket(rel):
    half = N_BUCKETS // 2
    ret = jnp.where(rel > 0, half, 0)
    n = jnp.abs(rel)
    max_exact = half // 2
    nf = jnp.maximum(n, 1).astype(jnp.float32)
    large = max_exact + (jnp.log(nf / max_exact) / math.log(MAX_DISTANCE / max_exact)
                         * (half - max_exact)).astype(jnp.int32)
    large = jnp.minimum(large, half - 1)
    return ret + jnp.where(n < max_exact, n, large)


def _near_bias_tables(rel_bias):
    T = ATT_TILE
    L = 2 * T
    H = rel_bias.shape[1]
    far = rel_bias[_t5_bucket(jnp.asarray(-MAX_DISTANCE))].astype(F32)
    m = jnp.arange(L)
    rel = jnp.where(m < T, -m, L - m)

    def toeplitz(rels):
        v = ((rel_bias[_t5_bucket(rels)].astype(F32) - far) * LOG2E).T
        rows = jnp.tile(v, (1, T))[:, :T * (L - 1)].reshape(H, T, L - 1)
        return rows[:, :, :T]

    kpos = jnp.arange(T)[:, None]
    qpos = jnp.arange(T)[None, :]
    visible = (kpos // CHUNK) <= (qpos // CHUNK)
    return jnp.where(visible[None], toeplitz(rel), -jnp.inf), toeplitz(rel - T)


def kernel(x, norm_mix_g, w_in, diff_lambda_q1, diff_lambda_k1, diff_lambda_q2, diff_lambda_k2, diff_subln_g,
           rel_bias, w_branch_diff, w_branch_sb, w_out, norm_ffn_g, w_ffn_up, ffn_conv_w, ffn_conv_b, w_ffn_down,
           norm_final_g):
    B, S, D = x.shape
    depth = w_in.shape[0]
    T = ATT_TILE
    n_heads_diff = rel_bias.shape[1]
    qk_w = n_heads_diff * 2 * HEAD_DIM
    ngd = qk_w // LANES
    assert S % T == 0 and T % CHUNK == 0 and T >= MAX_DISTANCE and B % DIFF_BATCH_BLOCK == 0
    assert (6 * qk_w) % D == 0
    d_ff = ffn_conv_w.shape[-1]

    near_diag, near_prev = _near_bias_tables(rel_bias)
    neg_upper = -(jnp.arange(T)[None, :] >= jnp.arange(T)[:, None]).astype(BF16)

    for layer in range(depth):
        lambda_init = 0.8 - 0.6 * math.exp(-0.3 * layer)
        gate_col = 6 * qk_w // D
        cast_weights = [(w_in, D, gate_col, 1), (w_in, D, gate_col + 1, 1), (w_branch_diff, D, 0, 1),
                        (w_branch_sb, D, 0, 1), (w_out, D, 0, 1), (w_ffn_up, 2 * d_ff, 0, 1),
                        (w_ffn_down, D, 0, 2)]
        (k, qbd, vt), (wgd, wgs, wbd, wbs, wo, wup, wdn) = _proj(
            x, norm_mix_g[layer][None], w_in, qk_w, tiles_per_step=2, layer=layer, cast_weights=cast_weights)

        row = lambda a: a[layer][None].astype(F32)
        y_diff = _diff_attention(qbd, k, vt, near_diag, near_prev, row(diff_lambda_q1), row(diff_lambda_k1),
                                 row(diff_lambda_q2), row(diff_lambda_k2), row(diff_subln_g), lambda_init, ngd, 0)
        y_sb = _sb_attention(qbd, k, vt, neg_upper, ngd, 1)

        x = _mix(x.reshape(B * S, D), y_diff.reshape(B * S, -1), y_sb.reshape(B * S, -1), norm_mix_g[layer][None],
                 wgd, wgs, wbd, wbs, wo, tm=1024).reshape(B, S, D)
        x = _ffn(x, norm_ffn_g[layer][None], wup, ffn_conv_w[layer], ffn_conv_b[layer][None], wdn,
                 norm_final_g[None], tm=512, final_norm=layer == depth - 1)
    return x
```

```python
import functools
import math

import jax
import jax.numpy as jnp
from jax import lax
from jax.experimental import pallas as pl
from jax.experimental.pallas import tpu as pltpu

F32 = jnp.float32
BF16 = jnp.bfloat16

HEAD_DIM = 64
CHUNK = 64
N_BUCKETS = 32
MAX_DISTANCE = 128
NORM_EPS = 1e-6
LOG2E = math.log2(math.e)
DEAD_LOG2 = -150.0

LANES = 128
SUBLANES = 8
BF16_ROWS = 16
ATT_TILE = 256
VMEM_LIMIT = 56 * 1024 * 1024


def _dot(a, b):
    return jnp.dot(a, b, preferred_element_type=F32)


def _rms(x, g):
    return x * lax.rsqrt(jnp.mean(x * x, axis=-1, keepdims=True) + NORM_EPS) * g


def _proj_kernel(x_ref, g_ref, dq_ref, dk_ref, dv_ref, sq_ref, sk_ref, sv_ref, *refs, n_groups, scale, n_cast):
    cast_in, (k_ref, qbd_ref, vt_ref), cast_out = refs[:n_cast], refs[n_cast:n_cast + 3], refs[n_cast + 3:n_cast * 2 + 3]
    wk_ref, wqvt_ref = refs[n_cast * 2 + 3:]
    T = ATT_TILE
    tiles = x_ref.shape[1] // T

    @pl.when(jnp.logical_and(pl.program_id(0) == 0, pl.program_id(1) == 0))
    def _():
        width = dk_ref.shape[1]
        for n, ref in enumerate((dk_ref, sk_ref)):
            wk_ref[:, n * width:(n + 1) * width] = ref[...].astype(BF16)
        for n, ref in enumerate((dq_ref, sq_ref, dv_ref, sv_ref)):
            wqvt_ref[n * width:(n + 1) * width, :] = ref[...].T.astype(BF16)

    h = _rms(x_ref[0], g_ref[...]).astype(BF16)
    kk = _dot(h, wk_ref[...]).astype(BF16)
    for g in range(n_groups):
        k_ref[0, g] = kk[:, g * LANES:(g + 1) * LANES]
    qv = lax.dot_general(wqvt_ref[...], h, (((1,), (1,)), ((), ())), preferred_element_type=F32)
    nq = n_groups * LANES
    first = lax.broadcasted_iota(jnp.int32, (LANES, T), 0) < HEAD_DIM
    zero = jnp.zeros((LANES, T), BF16)
    for t in range(tiles):
        for g in range(n_groups):
            qg = (qv[g * LANES:(g + 1) * LANES, t * T:(t + 1) * T] * scale).astype(BF16)
            qa, qb = jnp.where(first, qg, zero), jnp.where(first, zero, qg)
            if g < n_groups // 2:
                qbd_ref[0, t, g] = jnp.concatenate([qa, qb], axis=1)
            else:
                half = T // 2
                qbd_ref[0, t, g] = jnp.concatenate([qa[:, :half], qb[:, :half], qa[:, half:], qb[:, half:]], axis=1)
            vt_ref[0, t, g] = qv[nq + g * LANES:nq + (g + 1) * LANES, t * T:(t + 1) * T].astype(BF16)
    for src, dst in zip(cast_in, cast_out):
        dst[...] = src[...].astype(BF16)


def _proj(x, g, w_in, qk_w, tiles_per_step, layer, cast_weights):
    B, S, D = x.shape
    T = ATT_TILE
    nt = S // T
    tps = tiles_per_step
    steps_per_batch = nt // tps
    n_steps = B * steps_per_batch
    n_groups = 2 * qk_w // LANES
    w_block = lambda col: pl.BlockSpec((None, D, qk_w), lambda b, i: (layer, 0, col), pipeline_mode=pl.Buffered(1))
    cast_in_specs, cast_out_specs, cast_out_shapes = [], [], []
    for w, width, col, hold in cast_weights:
        n_rows = w.shape[1]
        rows = n_rows * hold // n_steps
        assert rows * n_steps == n_rows * hold and rows % BF16_ROWS == 0 and w.shape[2] % width == 0
        cast_in_specs.append(pl.BlockSpec(
            (None, rows, width), lambda b, i, col=col, hold=hold: (layer, (b * steps_per_batch + i) // hold, col)))
        cast_out_specs.append(pl.BlockSpec(
            (rows, width), lambda b, i, hold=hold: ((b * steps_per_batch + i) // hold, 0)))
        cast_out_shapes.append(jax.ShapeDtypeStruct((n_rows, width), BF16))
    kern = functools.partial(_proj_kernel, n_groups=n_groups, scale=HEAD_DIM ** -0.5 * LOG2E,
                             n_cast=len(cast_weights))
    outs = pl.pallas_call(
        kern,
        grid=(B, steps_per_batch),
        in_specs=[
            pl.BlockSpec((1, tps * T, D), lambda b, i: (b, i, 0)),
            pl.BlockSpec((1, D), lambda b, i: (0, 0)),
        ] + [w_block(col) for col in range(6)] + cast_in_specs,
        out_specs=[
            pl.BlockSpec((1, n_groups, tps * T, LANES), lambda b, i: (b, 0, i, 0)),
            pl.BlockSpec((1, tps, n_groups, LANES, 2 * T), lambda b, i: (b, i, 0, 0, 0)),
            pl.BlockSpec((1, tps, n_groups, LANES, T), lambda b, i: (b, i, 0, 0, 0)),
        ] + cast_out_specs,
        out_shape=[
            jax.ShapeDtypeStruct((B, n_groups, S, LANES), BF16),
            jax.ShapeDtypeStruct((B, nt, n_groups, LANES, 2 * T), BF16),
            jax.ShapeDtypeStruct((B, nt, n_groups, LANES, T), BF16),
        ] + cast_out_shapes,
        scratch_shapes=[pltpu.VMEM((D, 2 * qk_w), BF16), pltpu.VMEM((4 * qk_w, D), BF16)],
        compiler_params=pltpu.CompilerParams(
            dimension_semantics=("arbitrary", "arbitrary"), vmem_limit_bytes=VMEM_LIMIT),
        name="proj",
    )(x, g, *([w_in] * 6), *[w for w, _, _, _ in cast_weights])
    return outs[:3], outs[3:]


DIFF_BATCH_BLOCK = 2
SB_BATCH_BLOCK = 2


def _units(k_ref, n_groups):
    return [(bb, g) for bb in range(k_ref.shape[0]) for g in range(n_groups)]


def _key_tile(k_ref, bb, g, j):
    T = ATT_TILE
    start = j * T if isinstance(j, int) else pl.multiple_of(j * T, T)
    return k_ref[bb, g, pl.ds(start, T), :]


def _attention_specs(n_groups, group_block, S, nb):
    T = ATT_TILE
    gb = group_block
    return [
        pl.BlockSpec((nb, 1, n_groups, LANES, 2 * T), lambda b, i: (b, i, gb, 0, 0)),
        pl.BlockSpec((nb, n_groups, S, LANES), lambda b, i: (b, gb, 0, 0)),
        pl.BlockSpec((nb, S // T, n_groups, LANES, T), lambda b, i: (b, 0, gb, 0, 0)),
    ]


def _diff_kernel(lq1_ref, lk1_ref, lq2_ref, lk2_ref, gs_ref, nbd_ref, nbp_ref, qbd_ref, k_ref, vt_ref,
                 y_ref, z_ref, mx_ref, l_ref, ot_ref, *, n_groups, lambda_init):
    T = ATT_TILE
    qi = pl.program_id(1)
    units = list(enumerate(_units(k_ref, n_groups)))

    def score_unit(u, bb, g, j, slot, nb_ref):
        st = _dot(_key_tile(k_ref, bb, g, j), qbd_ref[bb, 0, g])
        if nb_ref is not None:
            nb = nb_ref[g]
            st = jnp.concatenate([st[:, :T] + nb, st[:, T:] + nb], axis=1)
        z_ref[slot, u] = st

    def tile(j, slot, next_slot=None, first=False):
        ps, alphas = [], []
        for u, _ in units:
            s = z_ref[slot, u]
            m = jnp.max(s, axis=0, keepdims=True)
            if not first:
                m_old = mx_ref[u]
                m = jnp.maximum(m_old, m)
                alphas.append(jnp.exp2(m_old - m))
            p = jnp.exp2(s - m)
            lsum = jnp.sum(p, axis=0, keepdims=True)
            l_ref[u] = lsum if first else alphas[u] * l_ref[u] + lsum
            mx_ref[u] = m
            ps.append(p.astype(BF16))
        for u, (bb, g) in units:
            if next_slot is not None:
                score_unit(u, bb, g, jnp.maximum(j - 1, 0), next_slot, None)
            pv = _dot(vt_ref[bb, j, g], ps[u])
            ot_ref[u] = pv if first else alphas[u] * ot_ref[u] + pv

    def diagonal_pair(prefetch):
        for u, (bb, g) in units:
            score_unit(u, bb, g, qi, 0, nbd_ref)
            score_unit(u, bb, g, qi - 1, 1, nbp_ref)
        ps = []
        for u, _ in units:
            s_d, s_p = z_ref[0, u], z_ref[1, u]
            m = jnp.maximum(jnp.max(s_d, axis=0, keepdims=True), jnp.max(s_p, axis=0, keepdims=True))
            p_d, p_p = jnp.exp2(s_d - m), jnp.exp2(s_p - m)
            l_ref[u] = jnp.sum(p_d, axis=0, keepdims=True) + jnp.sum(p_p, axis=0, keepdims=True)
            mx_ref[u] = m
            ps.append(jnp.concatenate([p_p.astype(BF16), p_d.astype(BF16)], axis=0))
        for u, (bb, g) in units:
            if prefetch:
                score_unit(u, bb, g, qi - 2, 2, None)
            v = jnp.concatenate([vt_ref[bb, qi - 1, g], vt_ref[bb, qi, g]], axis=1)
            ot_ref[u] = _dot(v, ps[u])

    @pl.when(qi == 0)
    def _():
        for u, (bb, g) in units:
            score_unit(u, bb, g, 0, 0, nbd_ref)
        tile(0, 0, first=True)

    @pl.when(qi == 1)
    def _():
        diagonal_pair(prefetch=False)

    @pl.when(qi > 1)
    def _():
        diagonal_pair(prefetch=True)

    rest = jnp.maximum(qi - 1, 0)
    even = jnp.bitwise_and(rest, 1) == 0
    ends_in_pair = jnp.logical_and(even, rest > 0)

    def pair(t, carry):
        j = qi - 2 - 2 * t
        tile(j, 2, next_slot=0)
        tile(j - 1, 0, next_slot=2)
        return carry
    lax.fori_loop(0, lax.shift_right_logical(rest, 1) - ends_in_pair.astype(jnp.int32), pair, 0)

    @pl.when(ends_in_pair)
    def _():
        tile(1, 2, next_slot=0)
        tile(0, 0)

    @pl.when(jnp.logical_not(even))
    def _():
        tile(0, 2)

    lam = (jnp.exp(jnp.sum(lq1_ref[...] * lk1_ref[...], keepdims=True))
           - jnp.exp(jnp.sum(lq2_ref[...] * lk2_ref[...], keepdims=True)) + lambda_init)
    for u, (bb, g) in units:
        ot = ot_ref[u] * (1.0 / l_ref[u])
        o = ot[:, :T] - lam * ot[:, T:]
        o = o * lax.rsqrt(jnp.mean(o * o, axis=0, keepdims=True) + NORM_EPS)
        y = o.T * gs_ref[...] * (1.0 - lambda_init)
        y_ref[bb, :, g * LANES:(g + 1) * LANES] = y.astype(y_ref.dtype)


def _diff_attention(qbd, k, vt, near_diag, near_prev, lq1, lk1, lq2, lk2, gs, lambda_init, n_groups, group_block):
    B, _, S, _ = k.shape
    T = ATT_TILE
    nt = S // T
    nb = DIFF_BATCH_BLOCK
    n_units = nb * n_groups
    kern = functools.partial(_diff_kernel, n_groups=n_groups, lambda_init=lambda_init)
    small = lambda a: pl.BlockSpec(a.shape, lambda b, i: (0,) * a.ndim)
    return pl.pallas_call(
        kern,
        grid=(B // nb, nt),
        in_specs=[small(lq1), small(lk1), small(lq2), small(lk2), small(gs), small(near_diag), small(near_prev)]
        + _attention_specs(n_groups, group_block, S, nb),
        out_specs=pl.BlockSpec((nb, T, n_groups * LANES), lambda b, i: (b, i, 0)),
        out_shape=jax.ShapeDtypeStruct((B, S, n_groups * LANES), BF16),
        scratch_shapes=[
            pltpu.VMEM((3, n_units, T, 2 * T), F32),
            pltpu.VMEM((n_units, 1, 2 * T), F32),
            pltpu.VMEM((n_units, 1, 2 * T), F32),
            pltpu.VMEM((n_units, LANES, 2 * T), F32),
        ],
        compiler_params=pltpu.CompilerParams(
            dimension_semantics=("arbitrary", "arbitrary"), vmem_limit_bytes=VMEM_LIMIT),
        name="diff_attn",
    )(lq1, lk1, lq2, lk2, gs, near_diag, near_prev, qbd, k, vt)


def _sb_kernel(nu_ref, qbd_ref, k_ref, vt_ref, y_ref, z_ref, c_ref, ot_ref, *, n_groups):
    T = ATT_TILE
    P = T // 2
    qi = pl.program_id(1)
    units = list(enumerate(_units(k_ref, n_groups)))
    kk = lax.broadcasted_iota(jnp.int32, (P, T), 0)
    qq = lax.broadcasted_iota(jnp.int32, (P, T), 1)
    before = kk < jnp.where(qq >= P, qq - P, qq)
    sign = jnp.uint32(0x80000000)

    def score_unit(u, bb, g, j, slot):
        z_ref[slot, u] = _dot(_key_tile(k_ref, bb, g, j), qbd_ref[bb, 0, g])

    def score_tile(j, slot):
        for u, (bb, g) in units:
            score_unit(u, bb, g, j, slot)

    def softplus2(z):
        neg_abs = lax.bitcast_convert_type(lax.bitcast_convert_type(z, jnp.uint32) | sign, F32)
        return jnp.maximum(z, 0.0) + jnp.log2(1.0 + jnp.exp2(neg_abs))

    def tile(j, slot, prefetch):
        sps = [softplus2(z_ref[slot, u]).astype(BF16) for u, _ in units]
        incls = []
        for u, (bb, g) in units:
            if prefetch:
                score_unit(u, bb, g, jnp.maximum(j - 1, 0), 1 - slot)
            incls.append(_dot(nu_ref[...], sps[u]))
        ws = []
        for u, _ in units:
            ws.append(jnp.exp2((z_ref[slot, u] + c_ref[u]) + incls[u]).astype(BF16))
            c_ref[u] += incls[u][0:1]
        for u, (bb, g) in units:
            ot_ref[u] += _dot(vt_ref[bb, j, g], ws[u])

    def diagonal(with_previous):
        for u, (bb, g) in units:
            q = qbd_ref[bb, 0, g]
            kt = _key_tile(k_ref, bb, g, qi)
            z_ref[0, u, :P, :] = _dot(kt[:P], q)
            z_ref[0, u, P:, T:] = _dot(kt[P:], q[:, T:])
        sp_lo, sp_hi = [], []
        for u, _ in units:
            st = softplus2(z_ref[0, u, :P, :])
            sb = jnp.where(before, softplus2(z_ref[0, u, P:, T:]), 0.0)
            sp_lo.append(jnp.where(before, st[:, :T], 0.0).astype(BF16))
            sp_hi.append(jnp.concatenate([st[:, T:], sb], axis=0).astype(BF16))
        incl_lo, incl_hi = [], []
        for u, (bb, g) in units:
            if with_previous:
                score_unit(u, bb, g, qi - 1, 1)
            incl_lo.append(_dot(nu_ref[:P, :P], sp_lo[u]))
            incl_hi.append(_dot(nu_ref[...], sp_hi[u]))
        if with_previous:
            sp_p = [softplus2(z_ref[1, u]).astype(BF16) for u, _ in units]
            incl_p = [_dot(nu_ref[...], sp_p[u]) for u, _ in units]
        for u, (bb, g) in units:
            zt, zb = z_ref[0, u, :P, :], z_ref[0, u, P:, T:]
            w_lo = jnp.where(before, jnp.exp2(zt[:, :T] + incl_lo[u]), 0.0).astype(BF16)
            w_hi = jnp.concatenate([jnp.exp2(zt[:, T:] + incl_hi[u][:P]),
                                    jnp.where(before, jnp.exp2(zb + incl_hi[u][P:]), 0.0)], axis=0).astype(BF16)
            c = jnp.concatenate([incl_lo[u][0:1], incl_hi[u][0:1]], axis=1)
            vd = vt_ref[bb, qi, g]
            pv = jnp.concatenate([_dot(vd[:, :P], w_lo), _dot(vd, w_hi)], axis=1)
            if with_previous:
                w_p = jnp.exp2((z_ref[1, u] + c) + incl_p[u]).astype(BF16)
                pv = pv + _dot(vt_ref[bb, qi - 1, g], w_p)
                c = c + incl_p[u][0:1]
            ot_ref[u] = pv
            c_ref[u] = c

    @pl.when(qi == 0)
    def _():
        diagonal(False)

    @pl.when(qi > 0)
    def _():
        diagonal(True)

    def alive():
        return jnp.max(c_ref[...]) >= DEAD_LOG2

    rest = jnp.maximum(qi - 1, 0)
    n_pairs = lax.shift_right_logical(rest, 1)

    def pair(carry):
        t, _ = carry
        j = qi - 2 - 2 * t
        score_tile(j, 0)
        tile(j, 0, prefetch=True)
        tile(j - 1, 1, prefetch=False)
        return t + 1, alive()
    _, go = lax.while_loop(lambda carry: jnp.logical_and(carry[0] < n_pairs, carry[1]), pair,
                           (jnp.int32(0), alive()))

    @pl.when(jnp.logical_and(jnp.bitwise_and(rest, 1) == 1, go))
    def _():
        score_tile(0, 0)
        tile(0, 0, prefetch=False)

    for u, (bb, g) in units:
        ot = ot_ref[u]
        o = jnp.concatenate([jnp.concatenate([ot[:HEAD_DIM, :P], ot[:HEAD_DIM, T:T + P]], axis=1),
                             jnp.concatenate([ot[HEAD_DIM:, P:T], ot[HEAD_DIM:, T + P:]], axis=1)], axis=0)
        y_ref[bb, :, g * LANES:(g + 1) * LANES] = o.T.astype(y_ref.dtype)


def _sb_attention(qbd, k, vt, nu, n_groups, group_block):
    B, _, S, _ = k.shape
    T = ATT_TILE
    nb = SB_BATCH_BLOCK
    n_units = nb * n_groups
    kern = functools.partial(_sb_kernel, n_groups=n_groups)
    return pl.pallas_call(
        kern,
        grid=(B // nb, S // T),
        in_specs=[pl.BlockSpec(nu.shape, lambda b, i: (0, 0))] + _attention_specs(n_groups, group_block, S, nb),
        out_specs=pl.BlockSpec((nb, T, n_groups * LANES), lambda b, i: (b, i, 0)),
        out_shape=jax.ShapeDtypeStruct((B, S, n_groups * LANES), BF16),
        scratch_shapes=[
            pltpu.VMEM((2, n_units, T, 2 * T), F32),
            pltpu.VMEM((n_units, 1, 2 * T), F32),
            pltpu.VMEM((n_units, LANES, 2 * T), F32),
        ],
        compiler_params=pltpu.CompilerParams(
            dimension_semantics=("arbitrary", "arbitrary"), vmem_limit_bytes=VMEM_LIMIT),
        name="sb_attn",
    )(nu, qbd, k, vt)


def _mix_kernel(x_ref, yd_ref, ys_ref, g_ref, wgd_ref, wgs_ref, wbd_ref, wbs_ref, wo_ref, o_ref):
    x = x_ref[...]
    h = _rms(x, g_ref[...]).astype(BF16)
    merged = (jax.nn.sigmoid(_dot(h, wgd_ref[...])) * _dot(yd_ref[...], wbd_ref[...])
              + jax.nn.sigmoid(_dot(h, wgs_ref[...])) * _dot(ys_ref[...], wbs_ref[...]))
    o_ref[...] = x + _dot(merged.astype(BF16), wo_ref[...])


def _mix(x2d, yd, ys, g, wgd, wgs, wbd, wbs, wo, tm):
    N, D = x2d.shape
    const = lambda a: pl.BlockSpec(a.shape, lambda i: (0, 0), pipeline_mode=pl.Buffered(1))
    row = lambda w: pl.BlockSpec((tm, w), lambda i: (i, 0))
    return pl.pallas_call(
        _mix_kernel,
        grid=(N // tm,),
        in_specs=[row(D), row(yd.shape[1]), row(ys.shape[1]),
                  const(g), const(wgd), const(wgs), const(wbd), const(wbs), const(wo)],
        out_specs=row(D),
        out_shape=jax.ShapeDtypeStruct((N, D), F32),
        compiler_params=pltpu.CompilerParams(dimension_semantics=("arbitrary",), vmem_limit_bytes=VMEM_LIMIT),
        name="mix",
    )(x2d, yd, ys, g, wgd, wgs, wbd, wbs, wo)


def _ffn_kernel(x_ref, g_ref, wup_ref, cw_ref, cb_ref, wdn_ref, gf_ref, o_ref, abuf_ref, *, final_norm):
    tm = x_ref.shape[1]
    F = cw_ref.shape[1]
    H = SUBLANES

    @pl.when(pl.program_id(1) == 0)
    def _():
        abuf_ref[0:H] = jnp.zeros((H, F), F32)

    x = x_ref[0]
    h = _rms(x, g_ref[...]).astype(BF16)
    up = _dot(h, wup_ref[...])
    a = up[:, :F]
    abuf_ref[H:H + tm] = a
    conv = (cw_ref[0:1] * abuf_ref[H - 2:H - 2 + tm] + cw_ref[1:2] * abuf_ref[H - 1:H - 1 + tm]
            + cw_ref[2:3] * a + cb_ref[...])
    abuf_ref[0:H] = abuf_ref[tm:tm + H]
    gelu = 0.5 * conv * (1.0 + lax.erf(conv * (2.0 ** -0.5)))
    act = gelu * up[:, F:]
    x = x + _dot(act.astype(BF16), wdn_ref[...])
    o_ref[0] = _rms(x, gf_ref[...]) if final_norm else x


def _ffn(x, g, wup, cw, cb, wdn, gf, tm, final_norm):
    B, S, D = x.shape
    F = cw.shape[1]
    const = lambda a: pl.BlockSpec(a.shape, lambda b, i: (0, 0), pipeline_mode=pl.Buffered(1))
    return pl.pallas_call(
        functools.partial(_ffn_kernel, final_norm=final_norm),
        grid=(B, S // tm),
        in_specs=[pl.BlockSpec((1, tm, D), lambda b, i: (b, i, 0)),
                  const(g), const(wup), const(cw), const(cb), const(wdn), const(gf)],
        out_specs=pl.BlockSpec((1, tm, D), lambda b, i: (b, i, 0)),
        out_shape=jax.ShapeDtypeStruct((B, S, D), F32),
        scratch_shapes=[pltpu.VMEM((tm + SUBLANES, F), F32)],
        compiler_params=pltpu.CompilerParams(
            dimension_semantics=("arbitrary", "arbitrary"), vmem_limit_bytes=VMEM_LIMIT),
        name="ffn",
    )(x, g, wup, cw, cb, wdn, gf)


def _t5_bucket(rel):
    half = N_BUCKETS // 2
    ret = jnp.where(rel > 0, half, 0)
    n = jnp.abs(rel)
    max_exact = half // 2
    nf = jnp.maximum(n, 1).astype(jnp.float32)
    large = max_exact + (jnp.log(nf / max_exact) / math.log(MAX_DISTANCE / max_exact)
                         * (half - max_exact)).astype(jnp.int32)
    large = jnp.minimum(large, half - 1)
    return ret + jnp.where(n < max_exact, n, large)


def _near_bias_tables(rel_bias):
    T = ATT_TILE
    L = 2 * T
    H = rel_bias.shape[1]
    far = rel_bias[_t5_bucket(jnp.asarray(-MAX_DISTANCE))].astype(F32)
    m = jnp.arange(L)
    rel = jnp.where(m < T, -m, L - m)

    def toeplitz(rels):
        v = ((rel_bias[_t5_bucket(rels)].astype(F32) - far) * LOG2E).T
        rows = jnp.tile(v, (1, T))[:, :T * (L - 1)].reshape(H, T, L - 1)
        return rows[:, :, :T]

    kpos = jnp.arange(T)[:, None]
    qpos = jnp.arange(T)[None, :]
    visible = (kpos // CHUNK) <= (qpos // CHUNK)
    return jnp.where(visible[None], toeplitz(rel), -jnp.inf), toeplitz(rel - T)


def kernel(x, norm_mix_g, w_in, diff_lambda_q1, diff_lambda_k1, diff_lambda_q2, diff_lambda_k2, diff_subln_g,
           rel_bias, w_branch_diff, w_branch_sb, w_out, norm_ffn_g, w_ffn_up, ffn_conv_w, ffn_conv_b, w_ffn_down,
           norm_final_g):
    B, S, D = x.shape
    depth = w_in.shape[0]
    T = ATT_TILE
    n_heads_diff = rel_bias.shape[1]
    qk_w = n_heads_diff * 2 * HEAD_DIM
    ngd = qk_w // LANES
    assert S % T == 0 and T % CHUNK == 0 and T >= MAX_DISTANCE and B % DIFF_BATCH_BLOCK == 0
    assert (6 * qk_w) % D == 0
    d_ff = ffn_conv_w.shape[-1]

    near_diag, near_prev = _near_bias_tables(rel_bias)
    neg_upper = -(jnp.arange(T)[None, :] >= jnp.arange(T)[:, None]).astype(BF16)

    for layer in range(depth):
        lambda_init = 0.8 - 0.6 * math.exp(-0.3 * layer)
        gate_col = 6 * qk_w // D
        cast_weights = [(w_in, D, gate_col, 1), (w_in, D, gate_col + 1, 1), (w_branch_diff, D, 0, 1),
                        (w_branch_sb, D, 0, 1), (w_out, D, 0, 1), (w_ffn_up, 2 * d_ff, 0, 1),
                        (w_ffn_down, D, 0, 2)]
        (k, qbd, vt), (wgd, wgs, wbd, wbs, wo, wup, wdn) = _proj(
            x, norm_mix_g[layer][None], w_in, qk_w, tiles_per_step=2, layer=layer, cast_weights=cast_weights)

        row = lambda a: a[layer][None].astype(F32)
        y_diff = _diff_attention(qbd, k, vt, near_diag, near_prev, row(diff_lambda_q1), row(diff_lambda_k1),
                                 row(diff_lambda_q2), row(diff_lambda_k2), row(diff_subln_g), lambda_init, ngd, 0)
        y_sb = _sb_attention(qbd, k, vt, neg_upper, ngd, 1)

        x = _mix(x.reshape(B * S, D), y_diff.reshape(B * S, -1), y_sb.reshape(B * S, -1), norm_mix_g[layer][None],
                 wgd, wgs, wbd, wbs, wo, tm=1024).reshape(B, S, D)
        x = _ffn(x, norm_ffn_g[layer][None], wup, ffn_conv_w[layer], ffn_conv_b[layer][None], wdn,
                 norm_final_g[None], tm=512, final_norm=layer == depth - 1)
    return x
```

```python
import functools
import math

import jax
import jax.numpy as jnp
from jax import lax
from jax.experimental import pallas as pl
from jax.experimental.pallas import tpu as pltpu

F32 = jnp.float32
BF16 = jnp.bfloat16

HEAD_DIM = 64
CHUNK = 64
N_BUCKETS = 32
MAX_DISTANCE = 128
NORM_EPS = 1e-6
LOG2E = math.log2(math.e)
DEAD_LOG2 = -150.0

LANES = 128
SUBLANES = 8
BF16_ROWS = 16
ATT_TILE = 256
VMEM_LIMIT = 60 * 1024 * 1024


def _dot(a, b):
    return jnp.dot(a, b, preferred_element_type=F32)


def _rms(x, g):
    return x * lax.rsqrt(jnp.mean(x * x, axis=-1, keepdims=True) + NORM_EPS) * g


def _proj_kernel(x_ref, g_ref, dq_ref, dk_ref, dv_ref, sq_ref, sk_ref, sv_ref, *refs, n_groups, scale, n_cast):
    cast_in, (k_ref, qbd_ref, vt_ref), cast_out = refs[:n_cast], refs[n_cast:n_cast + 3], refs[n_cast + 3:n_cast * 2 + 3]
    wk_ref, wqvt_ref = refs[n_cast * 2 + 3:]
    T = ATT_TILE
    tiles = x_ref.shape[1] // T

    @pl.when(jnp.logical_and(pl.program_id(0) == 0, pl.program_id(1) == 0))
    def _():
        width = dk_ref.shape[1]
        for n, ref in enumerate((dk_ref, sk_ref)):
            wk_ref[:, n * width:(n + 1) * width] = ref[...].astype(BF16)
        for n, ref in enumerate((dq_ref, sq_ref, dv_ref, sv_ref)):
            wqvt_ref[n * width:(n + 1) * width, :] = ref[...].T.astype(BF16)

    h = _rms(x_ref[0], g_ref[...]).astype(BF16)
    kk = _dot(h, wk_ref[...]).astype(BF16)
    for g in range(n_groups):
        k_ref[0, g] = kk[:, g * LANES:(g + 1) * LANES]
    qv = lax.dot_general(wqvt_ref[...], h, (((1,), (1,)), ((), ())), preferred_element_type=F32)
    nq = n_groups * LANES
    first = lax.broadcasted_iota(jnp.int32, (LANES, T), 0) < HEAD_DIM
    zero = jnp.zeros((LANES, T), BF16)
    for t in range(tiles):
        for g in range(n_groups):
            qg = (qv[g * LANES:(g + 1) * LANES, t * T:(t + 1) * T] * scale).astype(BF16)
            qa, qb = jnp.where(first, qg, zero), jnp.where(first, zero, qg)
            if g < n_groups // 2:
                qbd_ref[0, t, g] = jnp.concatenate([qa, qb], axis=1)
            else:
                half = T // 2
                qbd_ref[0, t, g] = jnp.concatenate([qa[:, :half], qb[:, :half], qa[:, half:], qb[:, half:]], axis=1)
            vt_ref[0, t, g] = qv[nq + g * LANES:nq + (g + 1) * LANES, t * T:(t + 1) * T].astype(BF16)
    for src, dst in zip(cast_in, cast_out):
        dst[...] = src[...].astype(BF16)


def _proj(x, g, w_in, qk_w, tiles_per_step, layer, cast_weights):
    B, S, D = x.shape
    T = ATT_TILE
    nt = S // T
    tps = tiles_per_step
    steps_per_batch = nt // tps
    n_steps = B * steps_per_batch
    n_groups = 2 * qk_w // LANES
    w_block = lambda col: pl.BlockSpec((None, D, qk_w), lambda b, i: (layer, 0, col), pipeline_mode=pl.Buffered(1))
    cast_in_specs, cast_out_specs, cast_out_shapes = [], [], []
    for w, width, col, hold in cast_weights:
        n_rows = w.shape[1]
        rows = n_rows * hold // n_steps
        assert rows * n_steps == n_rows * hold and rows % BF16_ROWS == 0 and w.shape[2] % width == 0
        cast_in_specs.append(pl.BlockSpec(
            (None, rows, width), lambda b, i, col=col, hold=hold: (layer, (b * steps_per_batch + i) // hold, col)))
        cast_out_specs.append(pl.BlockSpec(
            (rows, width), lambda b, i, hold=hold: ((b * steps_per_batch + i) // hold, 0)))
        cast_out_shapes.append(jax.ShapeDtypeStruct((n_rows, width), BF16))
    kern = functools.partial(_proj_kernel, n_groups=n_groups, scale=HEAD_DIM ** -0.5 * LOG2E,
                             n_cast=len(cast_weights))
    outs = pl.pallas_call(
        kern,
        grid=(B, steps_per_batch),
        in_specs=[
            pl.BlockSpec((1, tps * T, D), lambda b, i: (b, i, 0)),
            pl.BlockSpec((1, D), lambda b, i: (0, 0)),
        ] + [w_block(col) for col in range(6)] + cast_in_specs,
        out_specs=[
            pl.BlockSpec((1, n_groups, tps * T, LANES), lambda b, i: (b, 0, i, 0)),
            pl.BlockSpec((1, tps, n_groups, LANES, 2 * T), lambda b, i: (b, i, 0, 0, 0)),
            pl.BlockSpec((1, tps, n_groups, LANES, T), lambda b, i: (b, i, 0, 0, 0)),
        ] + cast_out_specs,
        out_shape=[
            jax.ShapeDtypeStruct((B, n_groups, S, LANES), BF16),
            jax.ShapeDtypeStruct((B, nt, n_groups, LANES, 2 * T), BF16),
            jax.ShapeDtypeStruct((B, nt, n_groups, LANES, T), BF16),
        ] + cast_out_shapes,
        scratch_shapes=[pltpu.VMEM((D, 2 * qk_w), BF16), pltpu.VMEM((4 * qk_w, D), BF16)],
        compiler_params=pltpu.CompilerParams(
            dimension_semantics=("arbitrary", "arbitrary"), vmem_limit_bytes=VMEM_LIMIT),
        name="proj",
    )(x, g, *([w_in] * 6), *[w for w, _, _, _ in cast_weights])
    return outs[:3], outs[3:]


DIFF_BATCH_BLOCK = 2
SB_BATCH_BLOCK = 2


def _units(k_ref, n_groups):
    return [(bb, g) for bb in range(k_ref.shape[0]) for g in range(n_groups)]


def _key_tile(k_ref, bb, g, j):
    T = ATT_TILE
    start = j * T if isinstance(j, int) else pl.multiple_of(j * T, T)
    return k_ref[bb, g, pl.ds(start, T), :]


def _attention_specs(n_groups, group_block, S, nb):
    T = ATT_TILE
    gb = group_block
    return [
        pl.BlockSpec((nb, 1, n_groups, LANES, 2 * T), lambda b, i: (b, i, gb, 0, 0)),
        pl.BlockSpec((nb, n_groups, S, LANES), lambda b, i: (b, gb, 0, 0)),
        pl.BlockSpec((nb, S // T, n_groups, LANES, T), lambda b, i: (b, 0, gb, 0, 0)),
    ]


def _diff_kernel(lq1_ref, lk1_ref, lq2_ref, lk2_ref, gs_ref, nbd_ref, nbp_ref, qbd_ref, k_ref, vt_ref,
                 y_ref, z_ref, mx_ref, l_ref, ot_ref, *, n_groups, lambda_init):
    T = ATT_TILE
    qi = pl.program_id(1)
    units = list(enumerate(_units(k_ref, n_groups)))

    def score_unit(u, bb, g, j, slot, nb_ref):
        st = _dot(_key_tile(k_ref, bb, g, j), qbd_ref[bb, 0, g])
        if nb_ref is not None:
            nb = nb_ref[g]
            st = jnp.concatenate([st[:, :T] + nb, st[:, T:] + nb], axis=1)
        z_ref[slot, u] = st

    def tile(j, slot, next_slot=None, first=False):
        ps, alphas = [], []
        for u, _ in units:
            s = z_ref[slot, u]
            m = jnp.max(s, axis=0, keepdims=True)
            if not first:
                m_old = mx_ref[u]
                m = jnp.maximum(m_old, m)
                alphas.append(jnp.exp2(m_old - m))
            p = jnp.exp2(s - m)
            lsum = jnp.sum(p, axis=0, keepdims=True)
            l_ref[u] = lsum if first else alphas[u] * l_ref[u] + lsum
            mx_ref[u] = m
            ps.append(p.astype(BF16))
        for u, (bb, g) in units:
            if next_slot is not None:
                score_unit(u, bb, g, jnp.maximum(j - 1, 0), next_slot, None)
            pv = _dot(vt_ref[bb, j, g], ps[u])
            ot_ref[u] = pv if first else alphas[u] * ot_ref[u] + pv

    def diagonal_pair(prefetch):
        for u, (bb, g) in units:
            score_unit(u, bb, g, qi, 0, nbd_ref)
            score_unit(u, bb, g, qi - 1, 1, nbp_ref)
        ps = []
        for u, _ in units:
            s_d, s_p = z_ref[0, u], z_ref[1, u]
            m = jnp.maximum(jnp.max(s_d, axis=0, keepdims=True), jnp.max(s_p, axis=0, keepdims=True))
            p_d, p_p = jnp.exp2(s_d - m), jnp.exp2(s_p - m)
            l_ref[u] = jnp.sum(p_d, axis=0, keepdims=True) + jnp.sum(p_p, axis=0, keepdims=True)
            mx_ref[u] = m
            ps.append(jnp.concatenate([p_p.astype(BF16), p_d.astype(BF16)], axis=0))
        for u, (bb, g) in units:
            if prefetch:
                score_unit(u, bb, g, qi - 2, 2, None)
            v = jnp.concatenate([vt_ref[bb, qi - 1, g], vt_ref[bb, qi, g]], axis=1)
            ot_ref[u] = _dot(v, ps[u])

    @pl.when(qi == 0)
    def _():
        for u, (bb, g) in units:
            score_unit(u, bb, g, 0, 0, nbd_ref)
        tile(0, 0, first=True)

    @pl.when(qi == 1)
    def _():
        diagonal_pair(prefetch=False)

    @pl.when(qi > 1)
    def _():
        diagonal_pair(prefetch=True)

    rest = jnp.maximum(qi - 1, 0)
    even = jnp.bitwise_and(rest, 1) == 0
    ends_in_pair = jnp.logical_and(even, rest > 0)

    def pair(t, carry):
        j = qi - 2 - 2 * t
        tile(j, 2, next_slot=0)
        tile(j - 1, 0, next_slot=2)
        return carry
    lax.fori_loop(0, lax.shift_right_logical(rest, 1) - ends_in_pair.astype(jnp.int32), pair, 0)

    @pl.when(ends_in_pair)
    def _():
        tile(1, 2, next_slot=0)
        tile(0, 0)

    @pl.when(jnp.logical_not(even))
    def _():
        tile(0, 2)

    lam = (jnp.exp(jnp.sum(lq1_ref[...] * lk1_ref[...], keepdims=True))
           - jnp.exp(jnp.sum(lq2_ref[...] * lk2_ref[...], keepdims=True)) + lambda_init)
    for u, (bb, g) in units:
        ot = ot_ref[u] * (1.0 / l_ref[u])
        o = ot[:, :T] - lam * ot[:, T:]
        o = o * lax.rsqrt(jnp.mean(o * o, axis=0, keepdims=True) + NORM_EPS)
        y = o.T * gs_ref[...] * (1.0 - lambda_init)
        y_ref[bb, :, g * LANES:(g + 1) * LANES] = y.astype(y_ref.dtype)


def _diff_attention(qbd, k, vt, near_diag, near_prev, lq1, lk1, lq2, lk2, gs, lambda_init, n_groups, group_block):
    B, _, S, _ = k.shape
    T = ATT_TILE
    nt = S // T
    nb = DIFF_BATCH_BLOCK
    n_units = nb * n_groups
    kern = functools.partial(_diff_kernel, n_groups=n_groups, lambda_init=lambda_init)
    small = lambda a: pl.BlockSpec(a.shape, lambda b, i: (0,) * a.ndim)
    return pl.pallas_call(
        kern,
        grid=(B // nb, nt),
        in_specs=[small(lq1), small(lk1), small(lq2), small(lk2), small(gs), small(near_diag), small(near_prev)]
        + _attention_specs(n_groups, group_block, S, nb),
        out_specs=pl.BlockSpec((nb, T, n_groups * LANES), lambda b, i: (b, i, 0)),
        out_shape=jax.ShapeDtypeStruct((B, S, n_groups * LANES), BF16),
        scratch_shapes=[
            pltpu.VMEM((3, n_units, T, 2 * T), F32),
            pltpu.VMEM((n_units, 1, 2 * T), F32),
            pltpu.VMEM((n_units, 1, 2 * T), F32),
            pltpu.VMEM((n_units, LANES, 2 * T), F32),
        ],
        compiler_params=pltpu.CompilerParams(
            dimension_semantics=("arbitrary", "arbitrary"), vmem_limit_bytes=VMEM_LIMIT),
        name="diff_attn",
    )(lq1, lk1, lq2, lk2, gs, near_diag, near_prev, qbd, k, vt)


def _sb_kernel(nu_ref, qbd_ref, k_ref, vt_ref, y_ref, z_ref, c_ref, ot_ref, *, n_groups):
    T = ATT_TILE
    P = T // 2
    qi = pl.program_id(1)
    units = list(enumerate(_units(k_ref, n_groups)))
    kk = lax.broadcasted_iota(jnp.int32, (P, T), 0)
    qq = lax.broadcasted_iota(jnp.int32, (P, T), 1)
    before = kk < jnp.where(qq >= P, qq - P, qq)
    sign = jnp.uint32(0x80000000)

    def score_unit(u, bb, g, j, slot):
        z_ref[slot, u] = _dot(_key_tile(k_ref, bb, g, j), qbd_ref[bb, 0, g])

    def score_tile(j, slot):
        for u, (bb, g) in units:
            score_unit(u, bb, g, j, slot)

    def softplus2(z):
        neg_abs = lax.bitcast_convert_type(lax.bitcast_convert_type(z, jnp.uint32) | sign, F32)
        return jnp.maximum(z, 0.0) + jnp.log2(1.0 + jnp.exp2(neg_abs))

    def tile(j, slot, prefetch):
        sps = [softplus2(z_ref[slot, u]).astype(BF16) for u, _ in units]
        incls = []
        for u, (bb, g) in units:
            if prefetch:
                score_unit(u, bb, g, jnp.maximum(j - 1, 0), 1 - slot)
            incls.append(_dot(nu_ref[...], sps[u]))
        ws = []
        for u, _ in units:
            ws.append(jnp.exp2((z_ref[slot, u] + c_ref[u]) + incls[u]).astype(BF16))
            c_ref[u] += incls[u][0:1]
        for u, (bb, g) in units:
            ot_ref[u] += _dot(vt_ref[bb, j, g], ws[u])

    def diagonal(with_previous):
        for u, (bb, g) in units:
            q = qbd_ref[bb, 0, g]
            kt = _key_tile(k_ref, bb, g, qi)
            z_ref[0, u, :P, :] = _dot(kt[:P], q)
            z_ref[0, u, P:, T:] = _dot(kt[P:], q[:, T:])
        sp_lo, sp_hi = [], []
        for u, _ in units:
            st = softplus2(z_ref[0, u, :P, :])
            sb = jnp.where(before, softplus2(z_ref[0, u, P:, T:]), 0.0)
            sp_lo.append(jnp.where(before, st[:, :T], 0.0).astype(BF16))
            sp_hi.append(jnp.concatenate([st[:, T:], sb], axis=0).astype(BF16))
        incl_lo, incl_hi = [], []
        for u, (bb, g) in units:
            if with_previous:
                score_unit(u, bb, g, qi - 1, 1)
            incl_lo.append(_dot(nu_ref[:P, :P], sp_lo[u]))
            incl_hi.append(_dot(nu_ref[...], sp_hi[u]))
        if with_previous:
            sp_p = [softplus2(z_ref[1, u]).astype(BF16) for u, _ in units]
            incl_p = [_dot(nu_ref[...], sp_p[u]) for u, _ in units]
        for u, (bb, g) in units:
            zt, zb = z_ref[0, u, :P, :], z_ref[0, u, P:, T:]
            w_lo = jnp.where(before, jnp.exp2(zt[:, :T] + incl_lo[u]), 0.0).astype(BF16)
            w_hi = jnp.concatenate([jnp.exp2(zt[:, T:] + incl_hi[u][:P]),
                                    jnp.where(before, jnp.exp2(zb + incl_hi[u][P:]), 0.0)], axis=0).astype(BF16)
            c = jnp.concatenate([incl_lo[u][0:1], incl_hi[u][0:1]], axis=1)
            vd = vt_ref[bb, qi, g]
            pv = jnp.concatenate([_dot(vd[:, :P], w_lo), _dot(vd, w_hi)], axis=1)
            if with_previous:
                w_p = jnp.exp2((z_ref[1, u] + c) + incl_p[u]).astype(BF16)
                pv = pv + _dot(vt_ref[bb, qi - 1, g], w_p)
                c = c + incl_p[u][0:1]
            ot_ref[u] = pv
            c_ref[u] = c

    @pl.when(qi == 0)
    def _():
        diagonal(False)

    @pl.when(qi > 0)
    def _():
        diagonal(True)

    def alive():
        return jnp.max(c_ref[...]) >= DEAD_LOG2

    rest = jnp.maximum(qi - 1, 0)
    n_pairs = lax.shift_right_logical(rest, 1)

    def pair(carry):
        t, _ = carry
        j = qi - 2 - 2 * t
        score_tile(j, 0)
        tile(j, 0, prefetch=True)
        tile(j - 1, 1, prefetch=False)
        return t + 1, alive()
    _, go = lax.while_loop(lambda carry: jnp.logical_and(carry[0] < n_pairs, carry[1]), pair,
                           (jnp.int32(0), alive()))

    @pl.when(jnp.logical_and(jnp.bitwise_and(rest, 1) == 1, go))
    def _():
        score_tile(0, 0)
        tile(0, 0, prefetch=False)

    for u, (bb, g) in units:
        ot = ot_ref[u]
        o = jnp.concatenate([jnp.concatenate([ot[:HEAD_DIM, :P], ot[:HEAD_DIM, T:T + P]], axis=1),
                             jnp.concatenate([ot[HEAD_DIM:, P:T], ot[HEAD_DIM:, T + P:]], axis=1)], axis=0)
        y_ref[bb, :, g * LANES:(g + 1) * LANES] = o.T.astype(y_ref.dtype)


def _sb_attention(qbd, k, vt, nu, n_groups, group_block):
    B, _, S, _ = k.shape
    T = ATT_TILE
    nb = SB_BATCH_BLOCK
    n_units = nb * n_groups
    kern = functools.partial(_sb_kernel, n_groups=n_groups)
    return pl.pallas_call(
        kern,
        grid=(B // nb, S // T),
        in_specs=[pl.BlockSpec(nu.shape, lambda b, i: (0, 0))] + _attention_specs(n_groups, group_block, S, nb),
        out_specs=pl.BlockSpec((nb, T, n_groups * LANES), lambda b, i: (b, i, 0)),
        out_shape=jax.ShapeDtypeStruct((B, S, n_groups * LANES), BF16),
        scratch_shapes=[
            pltpu.VMEM((2, n_units, T, 2 * T), F32),
            pltpu.VMEM((n_units, 1, 2 * T), F32),
            pltpu.VMEM((n_units, LANES, 2 * T), F32),
        ],
        compiler_params=pltpu.CompilerParams(
            dimension_semantics=("arbitrary", "arbitrary"), vmem_limit_bytes=VMEM_LIMIT),
        name="sb_attn",
    )(nu, qbd, k, vt)


def _mix_kernel(x_ref, yd_ref, ys_ref, g_ref, wgd_ref, wgs_ref, wbd_ref, wbs_ref, wo_ref, o_ref):
    x = x_ref[...]
    h = _rms(x, g_ref[...]).astype(BF16)
    merged = (jax.nn.sigmoid(_dot(h, wgd_ref[...])) * _dot(yd_ref[...], wbd_ref[...])
              + jax.nn.sigmoid(_dot(h, wgs_ref[...])) * _dot(ys_ref[...], wbs_ref[...]))
    o_ref[...] = x + _dot(merged.astype(BF16), wo_ref[...])


def _mix(x2d, yd, ys, g, wgd, wgs, wbd, wbs, wo, tm):
    N, D = x2d.shape
    const = lambda a: pl.BlockSpec(a.shape, lambda i: (0, 0), pipeline_mode=pl.Buffered(1))
    row = lambda w: pl.BlockSpec((tm, w), lambda i: (i, 0))
    return pl.pallas_call(
        _mix_kernel,
        grid=(N // tm,),
        in_specs=[row(D), row(yd.shape[1]), row(ys.shape[1]),
                  const(g), const(wgd), const(wgs), const(wbd), const(wbs), const(wo)],
        out_specs=row(D),
        out_shape=jax.ShapeDtypeStruct((N, D), F32),
        compiler_params=pltpu.CompilerParams(dimension_semantics=("arbitrary",), vmem_limit_bytes=VMEM_LIMIT),
        name="mix",
    )(x2d, yd, ys, g, wgd, wgs, wbd, wbs, wo)


def _ffn_kernel(x_ref, g_ref, wup_ref, cw_ref, cb_ref, wdn_ref, gf_ref, o_ref, abuf_ref, *, final_norm):
    tm = x_ref.shape[1]
    F = cw_ref.shape[1]
    H = SUBLANES

    @pl.when(pl.program_id(1) == 0)
    def _():
        abuf_ref[0:H] = jnp.zeros((H, F), F32)

    x = x_ref[0]
    h = _rms(x, g_ref[...]).astype(BF16)
    up = _dot(h, wup_ref[...])
    a = up[:, :F]
    abuf_ref[H:H + tm] = a
    conv = (cw_ref[0:1] * abuf_ref[H - 2:H - 2 + tm] + cw_ref[1:2] * abuf_ref[H - 1:H - 1 + tm]
            + cw_ref[2:3] * a + cb_ref[...])
    abuf_ref[0:H] = abuf_ref[tm:tm + H]
    gelu = 0.5 * conv * (1.0 + lax.erf(conv * (2.0 ** -0.5)))
    act = gelu * up[:, F:]
    x = x + _dot(act.astype(BF16), wdn_ref[...])
    o_ref[0] = _rms(x, gf_ref[...]) if final_norm else x


def _ffn(x, g, wup, cw, cb, wdn, gf, tm, final_norm):
    B, S, D = x.shape
    F = cw.shape[1]
    const = lambda a: pl.BlockSpec(a.shape, lambda b, i: (0, 0), pipeline_mode=pl.Buffered(1))
    return pl.pallas_call(
        functools.partial(_ffn_kernel, final_norm=final_norm),
        grid=(B, S // tm),
        in_specs=[pl.BlockSpec((1, tm, D), lambda b, i: (b, i, 0)),
                  const(g), const(wup), const(cw), const(cb), const(wdn), const(gf)],
        out_specs=pl.BlockSpec((1, tm, D), lambda b, i: (b, i, 0)),
        out_shape=jax.ShapeDtypeStruct((B, S, D), F32),
        scratch_shapes=[pltpu.VMEM((tm + SUBLANES, F), F32)],
        compiler_params=pltpu.CompilerParams(
            dimension_semantics=("arbitrary", "arbitrary"), vmem_limit_bytes=VMEM_LIMIT),
        name="ffn",
    )(x, g, wup, cw, cb, wdn, gf)


def _t5_bucket(rel):
    half = N_BUCKETS // 2
    ret = jnp.where(rel > 0, half, 0)
    n = jnp.abs(rel)
    max_exact = half // 2
    nf = jnp.maximum(n, 1).astype(jnp.float32)
    large = max_exact + (jnp.log(nf / max_exact) / math.log(MAX_DISTANCE / max_exact)
                         * (half - max_exact)).astype(jnp.int32)
    large = jnp.minimum(large, half - 1)
    return ret + jnp.where(n < max_exact, n, large)


def _near_bias_tables(rel_bias):
    T = ATT_TILE
    L = 2 * T
    H = rel_bias.shape[1]
    far = rel_bias[_t5_bucket(jnp.asarray(-MAX_DISTANCE))].astype(F32)
    m = jnp.arange(L)
    rel = jnp.where(m < T, -m, L - m)

    def toeplitz(rels):
        v = ((rel_bias[_t5_bucket(rels)].astype(F32) - far) * LOG2E).T
        rows = jnp.tile(v, (1, T))[:, :T * (L - 1)].reshape(H, T, L - 1)
        return rows[:, :, :T]

    kpos = jnp.arange(T)[:, None]
    qpos = jnp.arange(T)[None, :]
    visible = (kpos // CHUNK) <= (qpos // CHUNK)
    return jnp.where(visible[None], toeplitz(rel), -jnp.inf), toeplitz(rel - T)


def kernel(x, norm_mix_g, w_in, diff_lambda_q1, diff_lambda_k1, diff_lambda_q2, diff_lambda_k2, diff_subln_g,
           rel_bias, w_branch_diff, w_branch_sb, w_out, norm_ffn_g, w_ffn_up, ffn_conv_w, ffn_conv_b, w_ffn_down,
           norm_final_g):
    B, S, D = x.shape
    depth = w_in.shape[0]
    T = ATT_TILE
    n_heads_diff = rel_bias.shape[1]
    qk_w = n_heads_diff * 2 * HEAD_DIM
    ngd = qk_w // LANES
    assert S % T == 0 and T % CHUNK == 0 and T >= MAX_DISTANCE and B % DIFF_BATCH_BLOCK == 0
    assert (6 * qk_w) % D == 0
    d_ff = ffn_conv_w.shape[-1]

    near_diag, near_prev = _near_bias_tables(rel_bias)
    neg_upper = -(jnp.arange(T)[None, :] >= jnp.arange(T)[:, None]).astype(BF16)

    for layer in range(depth):
        lambda_init = 0.8 - 0.6 * math.exp(-0.3 * layer)
        gate_col = 6 * qk_w // D
        cast_weights = [(w_in, D, gate_col, 1), (w_in, D, gate_col + 1, 1), (w_branch_diff, D, 0, 1),
                        (w_branch_sb, D, 0, 1), (w_out, D, 0, 1), (w_ffn_up, 2 * d_ff, 0, 1),
                        (w_ffn_down, D, 0, 1)]
        (k, qbd, vt), (wgd, wgs, wbd, wbs, wo, wup, wdn) = _proj(
            x, norm_mix_g[layer][None], w_in, qk_w, tiles_per_step=4, layer=layer, cast_weights=cast_weights)

        row = lambda a: a[layer][None].astype(F32)
        y_diff = _diff_attention(qbd, k, vt, near_diag, near_prev, row(diff_lambda_q1), row(diff_lambda_k1),
                                 row(diff_lambda_q2), row(diff_lambda_k2), row(diff_subln_g), lambda_init, ngd, 0)
        y_sb = _sb_attention(qbd, k, vt, neg_upper, ngd, 1)

        x = _mix(x.reshape(B * S, D), y_diff.reshape(B * S, -1), y_sb.reshape(B * S, -1), norm_mix_g[layer][None],
                 wgd, wgs, wbd, wbs, wo, tm=1024).reshape(B, S, D)
        x = _ffn(x, norm_ffn_g[layer][None], wup, ffn_conv_w[layer], ffn_conv_b[layer][None], wdn,
                 norm_final_g[None], tm=512, final_norm=layer == depth - 1)
    return x
```
